```python
import math, functools
import jax, jax.numpy as jnp
from jax import lax
import numpy as np

D_MODEL = 1024
BATCH = 2
SEQ = 8192
DEPTH = 2
DEC_BATCH = 32
DEC_SEQ = 8
PAST_LEN = 16384
PAGE_SIZE = 128

N_META = 16
HEAD_DIM = 64
GDN_HEADS = 8
SB_HEADS = 8
GDN_WIDTH = GDN_HEADS * HEAD_DIM
SB_WIDTH = SB_HEADS * HEAD_DIM
MIX_WIDTH = GDN_WIDTH + SB_WIDTH
CONV_WIDTH = 4
GDN_CHUNK = 64
SB_BLOCK = 128
PEER_HEADS = 8
PEER_N_KEYS = 128
PEER_N_EXPERTS = PEER_N_KEYS * PEER_N_KEYS
PEER_KEY_DIM = 256
PEER_HALF = PEER_KEY_DIM // 2
PEER_TOPK = 16
PEER_BLOCK = 256
RMS_EPS = 1e-6

QKV_A = 3 * GDN_WIDTH
_OFF_GATE = QKV_A
_OFF_A = _OFF_GATE + GDN_WIDTH
_OFF_B = _OFF_A + GDN_HEADS
_OFF_QB = _OFF_B + GDN_HEADS
_OFF_KB = _OFF_QB + SB_WIDTH
_OFF_VB = _OFF_KB + SB_WIDTH
IN_WIDTH = _OFF_VB + SB_WIDTH
SPLITS = (_OFF_GATE, _OFF_A, _OFF_B, _OFF_QB, _OFF_KB, _OFF_VB)

kernel_name = 'hymba_gdn_stickbreak_peer_step'


def rmsnorm(x, w):
    xf = x.astype(jnp.float32)
    y = xf * lax.rsqrt(jnp.mean(xf * xf, axis=-1, keepdims=True) + RMS_EPS)
    return (y * w.astype(jnp.float32)).astype(x.dtype)


def l2norm(x):
    return x * lax.rsqrt(jnp.sum(x * x, axis=-1, keepdims=True) + 1e-6)


def gated_rmsnorm(o, gate, w):
    gate = gate.astype(jnp.float32).reshape(o.shape)
    on = o * lax.rsqrt(jnp.mean(o * o, axis=-1, keepdims=True) + RMS_EPS)
    return on * w.astype(jnp.float32) * jax.nn.silu(gate)


def causal_conv(x, buf, w):
    T = x.shape[1]
    xx = jnp.concatenate([buf.astype(x.dtype), x], axis=1)
    y = xx[:, 0:T] * w[0]
    for i in range(1, CONV_WIDTH):
        y = y + xx[:, i:i + T] * w[i]
    return y, xx[:, -(CONV_WIDTH - 1):]


def gdn_chunked(q, k, v, g, beta, S0, chunk):
    Bx, T, H, DK = q.shape
    DV = v.shape[-1]
    n = T // chunk

    def blocks(a):
        a = a.reshape((Bx, n, chunk) + a.shape[2:])
        return jnp.moveaxis(jnp.moveaxis(a, 1, 0), 2, 3)

    incl = jnp.tril(jnp.ones((chunk, chunk), bool))
    strict = jnp.tril(jnp.ones((chunk, chunk), bool), -1)
    eye = jnp.eye(chunk, dtype=jnp.float32)

    def step(S, inp):
        qc, kc, vc, gc, bc = inp
        G = jnp.cumsum(gc, axis=-1)
        decay = jnp.exp(jnp.where(incl, G[..., :, None] - G[..., None, :], -jnp.inf))
        kb = kc * bc[..., None]
        M = eye + jnp.where(strict, jnp.einsum('bhid,bhjd->bhij', kb, kc) * decay, 0.0)
        rhs = jnp.concatenate([vc * bc[..., None], kb * jnp.exp(G)[..., None]], axis=-1)
        sol = lax.linalg.triangular_solve(M, rhs, left_side=True, lower=True, unit_diagonal=True)
        u = sol[..., :DV] - jnp.einsum('bhck,bhkv->bhcv', sol[..., DV:], S)
        o = (jnp.einsum('bhck,bhkv->bhcv', qc * jnp.exp(G)[..., None], S)
             + jnp.einsum('bhij,bhjv->bhiv', jnp.einsum('bhid,bhjd->bhij', qc, kc) * decay, u))
        G_last = G[..., -1:]
        S = S * jnp.exp(G_last)[..., None] + jnp.einsum('bhck,bhcv->bhkv', kc * jnp.exp(G_last - G)[..., None], u)
        return S, o

    S, o = lax.scan(step, S0, (blocks(q), blocks(k), blocks(v), blocks(g), blocks(beta)))
    o = jnp.moveaxis(jnp.moveaxis(o, 3, 2), 0, 1).reshape(Bx, T, H, DV)
    return o, S


def gdn_prompt(q, k, v, g, beta):
    Bx = q.shape[0]
    S0 = jnp.zeros((Bx, GDN_HEADS, HEAD_DIM, HEAD_DIM), jnp.float32)
    m = slice(0, N_META)
    r = slice(N_META, None)
    o_m, S1 = gdn_chunked(q[:, m], k[:, m], v[:, m], g[:, m], beta[:, m], S0, N_META)
    o_r, S2 = gdn_chunked(q[:, r], k[:, r], v[:, r], g[:, r], beta[:, r], S1, GDN_CHUNK)
    return jnp.concatenate([o_m, o_r], axis=1), S2


def stick_breaking(q, k, v, bias, q_pos, k_pos):
    z = jnp.einsum('bqhd,bkhd->bhqk', q, k, preferred_element_type=jnp.float32) * (HEAD_DIM ** -0.5)
    z = z + bias.astype(jnp.float32)[None, :, None, None]
    mask = k_pos[None, :] < q_pos[:, None]
    log_stay = jnp.where(mask, jax.nn.log_sigmoid(-z), 0.0)
    after = lax.cumsum(log_stay, axis=3, reverse=True) - log_stay
    w = jnp.where(mask, jnp.exp(jax.nn.log_sigmoid(z) + after), 0.0)
    return jnp.einsum('bhqk,bkhd->bqhd', w, v.astype(jnp.float32))


def sb_prompt(q, k, v, bias):
    Bx, T, H, Dh = q.shape
    pos = jnp.arange(T)
    o_meta = stick_breaking(q[:, :N_META], k[:, :N_META], v[:, :N_META], bias, pos[:N_META], pos[:N_META])
    nb = (T - N_META) // SB_BLOCK
    qr = jnp.moveaxis(q[:, N_META:].reshape(Bx, nb, SB_BLOCK, H, Dh), 1, 0)
    pr = pos[N_META:].reshape(nb, SB_BLOCK)
    o_r = lax.map(lambda a: stick_breaking(a[0], k, v, bias, a[1], pos), (qr, pr))
    o_r = jnp.moveaxis(o_r, 0, 1).reshape(Bx, T - N_META, H, Dh)
    return jnp.concatenate([o_meta, o_r], axis=1)


def sb_sample(q, k, v, bias, k_past, v_past):
    T = q.shape[1]
    k_all = jnp.concatenate([k_past.astype(k.dtype), k], axis=1)
    v_all = jnp.concatenate([v_past.astype(v.dtype), v], axis=1)
    return stick_breaking(q, k_all, v_all, bias, PAST_LEN + jnp.arange(T), jnp.arange(PAST_LEN + T))


def token_mixers(x, ln1, w_in, conv_w, a_log, dt_bias, norm_w, sb_bias, w_out, conv_buf, gdn_fn, sb_fn):
    Bx, T, _ = x.shape
    f32 = jnp.float32
    h = rmsnorm(x, ln1)
    proj = jnp.dot(h, w_in)
    qkv_a, gate_a, a_a, b_a, q_b, k_b, v_b = jnp.split(proj, SPLITS, axis=-1)
    qkv_c, new_buf = causal_conv(qkv_a, conv_buf, conv_w)
    qkv_c = jax.nn.silu(qkv_c.astype(f32))
    q_a, k_a, v_a = jnp.split(qkv_c, 3, axis=-1)
    hs = (Bx, T, GDN_HEADS, HEAD_DIM)
    q_a = l2norm(q_a.reshape(hs)) * (HEAD_DIM ** -0.5)
    k_a = l2norm(k_a.reshape(hs))
    v_a = v_a.reshape(hs)
    g = -jnp.exp(a_log.astype(f32)) * jax.nn.softplus(a_a.astype(f32) + dt_bias.astype(f32))
    beta = jax.nn.sigmoid(b_a.astype(f32))
    o_a, S_new = gdn_fn(q_a, k_a, v_a, g, beta)
    o_a = gated_rmsnorm(o_a, gate_a, norm_w)
    sbs = (Bx, T, SB_HEADS, HEAD_DIM)
    k_b = k_b.reshape(sbs)
    v_b = v_b.reshape(sbs)
    o_b = sb_fn(q_b.reshape(sbs), k_b, v_b, sb_bias)
    o = jnp.concatenate([o_a.reshape(Bx, T, GDN_WIDTH), o_b.reshape(Bx, T, SB_WIDTH)], axis=-1).astype(x.dtype)
    return x + jnp.dot(o, w_out), S_new, new_buf, k_b, v_b


def _peer_block(h, wq, k1, k2, U, V):
    f32 = jnp.float32
    P = h.shape[0]
    q = jnp.dot(h, wq, preferred_element_type=f32).reshape(P, PEER_HEADS, PEER_KEY_DIM)
    s1 = jnp.einsum('phd,kd->phk', q[..., :PEER_HALF], k1.astype(f32))
    s2 = jnp.einsum('phd,kd->phk', q[..., PEER_HALF:], k2.astype(f32))
    v1, i1 = lax.top_k(s1, PEER_TOPK)
    v2, i2 = lax.top_k(s2, PEER_TOPK)
    n_cand = PEER_TOPK * PEER_TOPK
    cand_s = (v1[..., :, None] + v2[..., None, :]).reshape(P, PEER_HEADS, n_cand)
    cand_i = (i1[..., :, None] * PEER_N_KEYS + i2[..., None, :]).reshape(P, PEER_HEADS, n_cand)
    top_s, top_j = lax.top_k(cand_s, PEER_TOPK)
    idx = jnp.take_along_axis(cand_i, top_j, axis=-1)
    gate = jax.nn.softmax(top_s, axis=-1)
    act = jax.nn.gelu(jnp.einsum('phkd,pd->phk', U[idx], h, preferred_element_type=f32))
    out = jnp.einsum('phk,phkd->pd', (gate * act).astype(V.dtype), V[idx], preferred_element_type=f32)
    return out.astype(h.dtype)


def peer_ffn(h, wq, k1, k2, U, V):
    shape = h.shape
    hf = h.reshape(-1, D_MODEL)
    n = hf.shape[0]
    nb = -(-n // PEER_BLOCK)
    hp = jnp.pad(hf, ((0, nb * PEER_BLOCK - n), (0, 0))).reshape(nb, PEER_BLOCK, D_MODEL)
    out = lax.map(lambda hb: _peer_block(hb, wq, k1, k2, U, V), hp)
    return out.reshape(nb * PEER_BLOCK, D_MODEL)[:n].reshape(shape)


def setup_inputs(seed: int = 0) -> dict:
    key = jax.random.key(seed)
    ks = jax.random.split(key, 24)
    f32 = jnp.float32
    n_pages = PAST_LEN // PAGE_SIZE
    n_used = DEC_BATCH * n_pages
    n_pool = n_used + max(1, n_used // 4)
    nrm = jax.random.normal
    x_prompt = nrm(ks[0], (BATCH, SEQ, D_MODEL), f32)
    x_sample = nrm(ks[1], (DEC_BATCH, DEC_SEQ, D_MODEL), f32)
    cache_sb_k = nrm(ks[2], (DEPTH, n_pool, PAGE_SIZE, SB_HEADS, HEAD_DIM), f32)
    cache_sb_v = nrm(ks[3], (DEPTH, n_pool, PAGE_SIZE, SB_HEADS, HEAD_DIM), f32)
    page_table = jax.random.permutation(ks[4], n_pool)[:n_used].reshape(DEC_BATCH, n_pages).astype(jnp.int32)
    state_gdn = 0.5 * nrm(ks[5], (DEPTH, DEC_BATCH, GDN_HEADS, HEAD_DIM, HEAD_DIM), f32)
    state_conv = nrm(ks[6], (DEPTH, DEC_BATCH, CONV_WIDTH - 1, QKV_A), f32)
    meta_tokens = nrm(ks[7], (N_META, D_MODEL), f32)
    ln1_w = 1.0 + 0.02 * nrm(ks[8], (DEPTH, D_MODEL), f32)
    ln2_w = 1.0 + 0.02 * nrm(ks[9], (DEPTH, D_MODEL), f32)
    lnf_w = 1.0 + 0.02 * nrm(ks[10], (D_MODEL,), f32)
    w_in = nrm(ks[11], (DEPTH, D_MODEL, IN_WIDTH), f32) * (D_MODEL ** -0.5)
    conv_w = nrm(ks[12], (DEPTH, CONV_WIDTH, QKV_A), f32) * (CONV_WIDTH ** -0.5)
    gdn_a_log = jnp.log(jax.random.uniform(ks[13], (DEPTH, GDN_HEADS), f32, 1.0, 16.0))
    dt = jnp.exp(jax.random.uniform(ks[14], (DEPTH, GDN_HEADS), f32, math.log(1e-3), math.log(0.1)))
    gdn_dt_bias = dt + jnp.log(-jnp.expm1(-dt))
    gdn_norm_w = 1.0 + 0.02 * nrm(ks[15], (DEPTH, HEAD_DIM), f32)
    sb_logit_bias = jax.random.uniform(ks[22], (DEPTH, SB_HEADS), f32, -9.0, -4.0)
    w_out = nrm(ks[16], (DEPTH, MIX_WIDTH, D_MODEL), f32) * (MIX_WIDTH ** -0.5)
    peer_wq = nrm(ks[17], (DEPTH, D_MODEL, PEER_HEADS * PEER_KEY_DIM), f32) * (D_MODEL ** -0.5)
    peer_k1 = nrm(ks[18], (DEPTH, PEER_N_KEYS, PEER_HALF), f32) * (PEER_HALF ** -0.5)
    peer_k2 = nrm(ks[19], (DEPTH, PEER_N_KEYS, PEER_HALF), f32) * (PEER_HALF ** -0.5)
    peer_u = nrm(ks[20], (DEPTH, PEER_N_EXPERTS, D_MODEL), f32) * (D_MODEL ** -0.5)
    peer_v = nrm(ks[21], (DEPTH, PEER_N_EXPERTS, D_MODEL), f32) * 0.25
    return {'x_prompt': x_prompt, 'x_sample': x_sample, 'cache_sb_k': cache_sb_k, 'cache_sb_v': cache_sb_v,
            'page_table': page_table, 'state_gdn': state_gdn, 'state_conv': state_conv,
            'meta_tokens': meta_tokens, 'ln1_w': ln1_w, 'ln2_w': ln2_w, 'lnf_w': lnf_w, 'w_in': w_in,
            'conv_w': conv_w, 'gdn_a_log': gdn_a_log, 'gdn_dt_bias': gdn_dt_bias, 'gdn_norm_w': gdn_norm_w,
            'sb_logit_bias': sb_logit_bias, 'w_out': w_out, 'peer_wq': peer_wq, 'peer_k1': peer_k1,
            'peer_k2': peer_k2, 'peer_u': peer_u, 'peer_v': peer_v}


def reference(x_prompt, x_sample, cache_sb_k, cache_sb_v, page_table, state_gdn, state_conv,
              meta_tokens, ln1_w, ln2_w, lnf_w, w_in, conv_w, gdn_a_log, gdn_dt_bias, gdn_norm_w,
              sb_logit_bias, w_out, peer_wq, peer_k1, peer_k2, peer_u, peer_v):
    Bp = x_prompt.shape[0]
    Bs, Ts = x_sample.shape[0], x_sample.shape[1]
    n_pages = page_table.shape[1]
    meta = jnp.broadcast_to(meta_tokens.astype(x_prompt.dtype)[None], (Bp, N_META, D_MODEL))
    xp = jnp.concatenate([meta, x_prompt], axis=1)
    xs = x_sample
    kp_l, vp_l, ks_l, vs_l, Sp_l, Ss_l, cp_l, cs_l = [], [], [], [], [], [], [], []
    for l in range(DEPTH):
        conv0 = jnp.zeros((Bp, CONV_WIDTH - 1, QKV_A), xp.dtype)
        xp, Sp, cp, kp, vp = token_mixers(xp, ln1_w[l], w_in[l], conv_w[l], gdn_a_log[l], gdn_dt_bias[l],
                                          gdn_norm_w[l], sb_logit_bias[l], w_out[l], conv0,
                                          gdn_prompt, sb_prompt)
        xp = xp + peer_ffn(rmsnorm(xp, ln2_w[l]), peer_wq[l], peer_k1[l], peer_k2[l], peer_u[l], peer_v[l])
        k_past = cache_sb_k[l][page_table].reshape(Bs, n_pages * PAGE_SIZE, SB_HEADS, HEAD_DIM)
        v_past = cache_sb_v[l][page_table].reshape(Bs, n_pages * PAGE_SIZE, SB_HEADS, HEAD_DIM)
        gdn_fn = functools.partial(gdn_chunked, S0=state_gdn[l].astype(jnp.float32), chunk=Ts)
        sb_fn = functools.partial(sb_sample, k_past=k_past, v_past=v_past)
        xs, Ss, cs, ksn, vsn = token_mixers(xs, ln1_w[l], w_in[l], conv_w[l], gdn_a_log[l], gdn_dt_bias[l],
                                            gdn_norm_w[l], sb_logit_bias[l], w_out[l], state_conv[l],
                                            gdn_fn, sb_fn)
        xs = xs + peer_ffn(rmsnorm(xs, ln2_w[l]), peer_wq[l], peer_k1[l], peer_k2[l], peer_u[l], peer_v[l])
        kp_l.append(kp); vp_l.append(vp); ks_l.append(ksn); vs_l.append(vsn)
        Sp_l.append(Sp.astype(state_gdn.dtype)); Ss_l.append(Ss.astype(state_gdn.dtype))
        cp_l.append(cp); cs_l.append(cs)
    y_prompt = rmsnorm(xp, lnf_w)[:, N_META:]
    y_sample = rmsnorm(xs, lnf_w)
    return (y_prompt, y_sample, jnp.stack(kp_l), jnp.stack(vp_l), jnp.stack(ks_l), jnp.stack(vs_l),
            jnp.stack(Sp_l), jnp.stack(Ss_l), jnp.stack(cp_l), jnp.stack(cs_l))
```

```python
import functools

import jax
import jax.numpy as jnp
from jax import lax
from jax.experimental import pallas as pl
from jax.experimental.pallas import tpu as pltpu

F32 = jnp.float32
BF16 = jnp.bfloat16
HIGHEST = lax.Precision.HIGHEST

HEAD_DIM = 64
N_HEADS = 8
WIDTH = N_HEADS * HEAD_DIM
QKV_WIDTH = 3 * WIDTH
CONV_TAPS = 4
N_META = 16
GDN_CHUNK = 64
SB_BLOCK = 128
PAGE = 128
PEER_KEYS = 128
PEER_TOPK = 16
RMS_EPS = 1e-6
L2_EPS = 1e-6
LANES = 128
VMEM_LIMIT = 56 * 1024 * 1024


def _dot(a, b, precision=None):
    return jnp.dot(a, b, preferred_element_type=F32, precision=precision)


def _dot_nt(a, b, precision=None):
    return lax.dot_general(a, b, (((1,), (1,)), ((), ())),
                           preferred_element_type=F32, precision=precision)


def _params(*sem):
    return pltpu.CompilerParams(dimension_semantics=sem, vmem_limit_bytes=VMEM_LIMIT)


def _inproj_kernel(x_ref, lnw_ref, w_ref, qkv_ref, gate_ref, ab_ref, qb_ref,
                   kb_ref, vb_ref, kb16_ref, vb16_ref):
    x = x_ref[...]
    h = x * lax.rsqrt(jnp.mean(x * x, axis=-1, keepdims=True) + RMS_EPS) * lnw_ref[...]
    h = h.astype(BF16)
    o = 0
    qkv_ref[...] = _dot(h, w_ref[:, o:o + QKV_WIDTH]); o += QKV_WIDTH
    gate_ref[...] = _dot(h, w_ref[:, o:o + WIDTH]); o += WIDTH
    qb_ref[...] = (_dot(h, w_ref[:, o:o + WIDTH]) * (HEAD_DIM ** -0.5)).astype(BF16); o += WIDTH
    kb = _dot(h, w_ref[:, o:o + WIDTH]); o += WIDTH
    kb_ref[...] = kb
    kb16_ref[...] = kb.astype(BF16)
    vb = _dot(h, w_ref[:, o:o + WIDTH]); o += WIDTH
    vb_ref[...] = vb
    vb16_ref[...] = vb.astype(BF16)
    ab_ref[...] = _dot(h, w_ref[:, o:o + LANES])


def _inproj(x, lnw, w, tile):
    n, d = x.shape
    wcols = w.shape[1]
    row = lambda width: pl.BlockSpec((tile, width), lambda i: (i, 0))
    shp = lambda width, dt: jax.ShapeDtypeStruct((n, width), dt)
    return pl.pallas_call(
        _inproj_kernel,
        grid=(n // tile,),
        in_specs=[row(d), pl.BlockSpec((1, d), lambda i: (0, 0)),
                  pl.BlockSpec((d, wcols), lambda i: (0, 0))],
        out_specs=[row(QKV_WIDTH), row(WIDTH), row(LANES), row(WIDTH), row(WIDTH),
                   row(WIDTH), row(WIDTH), row(WIDTH)],
        out_shape=[shp(QKV_WIDTH, F32), shp(WIDTH, F32), shp(LANES, F32), shp(WIDTH, BF16),
                   shp(WIDTH, F32), shp(WIDTH, F32), shp(WIDTH, BF16), shp(WIDTH, BF16)],
        compiler_params=_params("parallel"),
        name="inproj",
    )(x, lnw, w)


def _silu(x):
    return x * (1.0 / (1.0 + jnp.exp(-x)))


def _softplus(x):
    return jnp.maximum(x, 0.0) + jnp.log1p(jnp.exp(-jnp.abs(x)))


def _unit_lower_inverse(low, chunk):
    ri = lax.broadcasted_iota(jnp.int32, (chunk, chunk), 0)
    ci = lax.broadcasted_iota(jnp.int32, (chunk, chunk), 1)
    inv = (ri == ci).astype(F32)
    s = 1
    while s < chunk:
        off = ((ri & s) != 0) & ((ci & s) == 0) & ((ri ^ ci) < 2 * s)
        blk = jnp.where(off, low, 0.0)
        inv = inv - _dot(inv, _dot(blk, inv, HIGHEST), HIGHEST)
        s *= 2
    return inv


def _gdn_kernel(qkv_ref, ab_ref, gate_ref, convw_ref, alog_ref, dtb_ref, normw_ref,
                s0_ref, conv0_ref, o_ref, sout_ref, convout_ref, xbuf, s_scr, *, chunk):
    c = pl.program_id(1)
    hist = CONV_TAPS - 1

    @pl.when(c == 0)
    def _():
        xbuf[0:8, :] = jnp.zeros((8, QKV_WIDTH), F32)
        xbuf[8 - hist:8, :] = conv0_ref[0]
        s_scr[...] = s0_ref[0]

    xbuf[8:8 + chunk, :] = qkv_ref[0]
    conv = convw_ref[0:1, :] * xbuf[8 - hist:8 - hist + chunk, :]
    for i in range(1, CONV_TAPS):
        conv = conv + convw_ref[i:i + 1, :] * xbuf[8 - hist + i:8 - hist + i + chunk, :]
    convout_ref[0] = xbuf[8 + chunk - hist:8 + chunk, :]
    tail = xbuf[chunk:chunk + 8, :]
    xbuf[0:8, :] = tail
    act = _silu(conv)

    ab = ab_ref[0]
    a_in = ab[:, 0:N_HEADS]
    b_in = ab[:, N_HEADS:2 * N_HEADS]
    g = -jnp.exp(alog_ref[...]) * _softplus(a_in + dtb_ref[...])
    beta = 1.0 / (1.0 + jnp.exp(-b_in))

    ri = lax.broadcasted_iota(jnp.int32, (chunk, chunk), 0)
    ci = lax.broadcasted_iota(jnp.int32, (chunk, chunk), 1)
    incl = ri >= ci
    strict = ri > ci
    tril = incl.astype(F32)
    gcum = _dot(tril, g, HIGHEST)

    for h in range(N_HEADS):
        sl = slice(h * HEAD_DIM, (h + 1) * HEAD_DIM)
        q = act[:, sl]
        k = act[:, WIDTH + h * HEAD_DIM:WIDTH + (h + 1) * HEAD_DIM]
        v = act[:, 2 * WIDTH + h * HEAD_DIM:2 * WIDTH + (h + 1) * HEAD_DIM]
        q = q * lax.rsqrt(jnp.sum(q * q, axis=-1, keepdims=True) + L2_EPS) * (HEAD_DIM ** -0.5)
        k = k * lax.rsqrt(jnp.sum(k * k, axis=-1, keepdims=True) + L2_EPS)
        gh = g[:, h:h + 1]
        bh = beta[:, h:h + 1]
        gc = gcum[:, h:h + 1]
        g_last = gcum[chunk - 1:chunk, h:h + 1]
        dlog = _dot(tril, jnp.where(strict, jnp.broadcast_to(gh, (chunk, chunk)), 0.0), HIGHEST)
        decay = jnp.where(incl, jnp.exp(dlog), 0.0)
        kbeta = k * bh
        low = jnp.where(strict, _dot_nt(kbeta, k, HIGHEST) * decay, 0.0)
        inv = _unit_lower_inverse(low, chunk)
        eg = jnp.exp(gc)
        state = s_scr[h]
        sol_v = _dot(inv, v * bh, HIGHEST)
        sol_k = _dot(inv, kbeta * eg, HIGHEST)
        u = sol_v - _dot(sol_k, state, HIGHEST)
        attn = _dot_nt(q, k, HIGHEST) * decay
        o = _dot(q * eg, state, HIGHEST) + _dot(attn, u, HIGHEST)
        kdec = k * jnp.exp(g_last - gc)
        s_scr[h] = state * jnp.exp(g_last) + lax.dot_general(
            kdec, u, (((0,), (0,)), ((), ())), preferred_element_type=F32, precision=HIGHEST)
        gate = gate_ref[0, :, sl]
        on = o * lax.rsqrt(jnp.mean(o * o, axis=-1, keepdims=True) + RMS_EPS)
        o_ref[0, :, sl] = (on * normw_ref[...] * _silu(gate)).astype(o_ref.dtype)

    sout_ref[0] = s_scr[...]


def _gdn(qkv, ab, gate, convw, alog, dtb, normw, s0, conv0, chunk):
    b, t, _ = qkv.shape
    n = t // chunk
    tok = lambda width: pl.BlockSpec((1, chunk, width), lambda i, c: (i, c, 0))
    const2 = lambda r, cdim: pl.BlockSpec((r, cdim), lambda i, c: (0, 0))
    hist = CONV_TAPS - 1
    return pl.pallas_call(
        functools.partial(_gdn_kernel, chunk=chunk),
        grid=(b, n),
        in_specs=[tok(QKV_WIDTH), tok(LANES), tok(WIDTH),
                  const2(CONV_TAPS, QKV_WIDTH), const2(1, N_HEADS), const2(1, N_HEADS),
                  const2(1, HEAD_DIM),
                  pl.BlockSpec((1, N_HEADS, HEAD_DIM, HEAD_DIM), lambda i, c: (i, 0, 0, 0)),
                  pl.BlockSpec((1, hist, QKV_WIDTH), lambda i, c: (i, 0, 0))],
        out_specs=[tok(WIDTH),
                   pl.BlockSpec((1, N_HEADS, HEAD_DIM, HEAD_DIM), lambda i, c: (i, 0, 0, 0)),
                   pl.BlockSpec((1, hist, QKV_WIDTH), lambda i, c: (i, 0, 0))],
        out_shape=[jax.ShapeDtypeStruct((b, t, WIDTH), BF16),
                   jax.ShapeDtypeStruct((b, N_HEADS, HEAD_DIM, HEAD_DIM), F32),
                   jax.ShapeDtypeStruct((b, hist, QKV_WIDTH), F32)],
        scratch_shapes=[pltpu.VMEM((chunk + 8, QKV_WIDTH), F32),
                        pltpu.VMEM((N_HEADS, HEAD_DIM, HEAD_DIM), F32)],
        compiler_params=_params("parallel", "arbitrary"),
        name=f"gdn_c{chunk}",
    )(qkv, ab, gate, convw, alog, dtb, normw, s0, conv0)


def _sb_weights(z, mask, carry, suffix):
    sp = jnp.log1p(jnp.exp(-jnp.abs(z)))
    log_beta = jnp.minimum(z, 0.0) - sp
    log_stay = jnp.where(mask, log_beta - z, 0.0)
    after = carry + _dot(log_stay.astype(BF16), suffix)
    w = jnp.where(mask, jnp.exp(log_beta + after), 0.0)
    return w, carry + jnp.sum(log_stay, axis=-1, keepdims=True)


def _suffix_matrix():
    r = lax.broadcasted_iota(jnp.int32, (LANES, LANES), 0)
    c = lax.broadcasted_iota(jnp.int32, (LANES, LANES), 1)
    return (r > c).astype(BF16)


def _sb_prompt_kernel(bias_ref, q_ref, k_ref, v_ref, o_ref, *, pad):
    h = pl.program_id(1)
    i = pl.program_id(2)
    q = q_ref[0, 0]
    bias = bias_ref[h]
    suffix = _suffix_matrix()
    ri = lax.broadcasted_iota(jnp.int32, (SB_BLOCK, SB_BLOCK), 0)
    ci = lax.broadcasted_iota(jnp.int32, (SB_BLOCK, SB_BLOCK), 1)

    def body(step, carry):
        acc, run = carry
        j = i - step
        start = pl.multiple_of(j * SB_BLOCK, SB_BLOCK)
        k = k_ref[0, 0, pl.ds(start, SB_BLOCK), :]
        v = v_ref[0, 0, pl.ds(start, SB_BLOCK), :]
        z = _dot_nt(q, k) + bias
        kpos = ci + j * SB_BLOCK
        mask = (kpos < ri + i * SB_BLOCK) & (kpos >= pad)
        w, run = _sb_weights(z, mask, run, suffix)
        return acc + _dot(w.astype(BF16), v), run

    acc, _ = lax.fori_loop(0, i + 1, body,
                           (jnp.zeros((SB_BLOCK, HEAD_DIM), F32), jnp.zeros((SB_BLOCK, 1), F32)))
    o_ref[0, 0] = acc.astype(o_ref.dtype)


def _sb_prompt(q, k, v, bias, pad):
    b, nh, t, d = q.shape
    nq = t // SB_BLOCK
    return pl.pallas_call(
        functools.partial(_sb_prompt_kernel, pad=pad),
        grid_spec=pltpu.PrefetchScalarGridSpec(
            num_scalar_prefetch=0,
            grid=(b, nh, nq),
            in_specs=[pl.BlockSpec(memory_space=pltpu.SMEM),
                      pl.BlockSpec((1, 1, SB_BLOCK, d), lambda bi, h, i: (bi, h, i, 0)),
                      pl.BlockSpec((1, 1, t, d), lambda bi, h, i: (bi, h, 0, 0)),
                      pl.BlockSpec((1, 1, t, d), lambda bi, h, i: (bi, h, 0, 0))],
            out_specs=pl.BlockSpec((1, 1, SB_BLOCK, d), lambda bi, h, i: (bi, h, i, 0)),
        ),
        out_shape=jax.ShapeDtypeStruct((b, nh, t, d), BF16),
        compiler_params=_params("parallel", "parallel", "arbitrary"),
        name="sb_prompt",
    )(bias, q, k, v)


def _sb_sample_kernel(pt_ref, qbd_ref, bias_ref, knew_ref, vnew_ref, *refs, pages_per_step, t_new):
    k_refs = refs[:pages_per_step]
    v_refs = refs[pages_per_step:2 * pages_per_step]
    o_ref = refs[2 * pages_per_step]
    acc_ref, run_ref = refs[2 * pages_per_step + 1:]
    g = pl.program_id(1)
    rows = N_HEADS * t_new
    qbd = qbd_ref[0]
    bias = bias_ref[...]
    suffix = _suffix_matrix()

    def visit(kblk, vblk, mask):
        z = _dot_nt(qbd, kblk) + bias
        w, run = _sb_weights(z, mask, run_ref[...], suffix)
        run_ref[...] = run
        acc_ref[...] += _dot(w.astype(BF16), vblk)

    @pl.when(g == 0)
    def _():
        acc_ref[...] = jnp.zeros_like(acc_ref)
        run_ref[...] = jnp.zeros_like(run_ref)
        ri = lax.broadcasted_iota(jnp.int32, (rows, LANES), 0)
        ci = lax.broadcasted_iota(jnp.int32, (rows, LANES), 1)
        visit(knew_ref[0], vnew_ref[0], ci < lax.rem(ri, t_new))

    everything = jnp.ones((rows, LANES), jnp.bool_)
    for p in reversed(range(pages_per_step)):
        visit(k_refs[p][0, 0].astype(BF16), v_refs[p][0, 0].astype(BF16), everything)

    @pl.when(g == pl.num_programs(1) - 1)
    def _():
        acc = acc_ref[...]
        lane_head = lax.broadcasted_iota(jnp.int32, (t_new, WIDTH), 1) // HEAD_DIM
        out = jnp.zeros((t_new, WIDTH), F32)
        for h in range(N_HEADS):
            out = out + jnp.where(lane_head == h, acc[h * t_new:(h + 1) * t_new, :], 0.0)
        o_ref[0] = out.astype(o_ref.dtype)


def _sb_sample(page_table, qbd, bias_rows, knew, vnew, cache_k, cache_v, layer, pages_per_step):
    b, rows, _ = qbd.shape
    t_new = rows // N_HEADS
    n_pages = page_table.shape[1]
    n_groups = n_pages // pages_per_step

    def page_spec(p):
        def index(bi, g, pt):
            return (layer, pt[bi, (n_groups - 1 - g) * pages_per_step + p], 0, 0)
        return pl.BlockSpec((1, 1, PAGE, WIDTH), index)

    per_batch = lambda r, c: pl.BlockSpec((1, r, c), lambda bi, g, pt: (bi, 0, 0))
    return pl.pallas_call(
        functools.partial(_sb_sample_kernel, pages_per_step=pages_per_step, t_new=t_new),
        grid_spec=pltpu.PrefetchScalarGridSpec(
            num_scalar_prefetch=1,
            grid=(b, n_groups),
            in_specs=[per_batch(rows, WIDTH),
                      pl.BlockSpec((rows, LANES), lambda bi, g, pt: (0, 0)),
                      per_batch(PAGE, WIDTH), per_batch(PAGE, WIDTH)]
                     + [page_spec(p) for p in range(pages_per_step)] * 2,
            out_specs=per_batch(t_new, WIDTH),
            scratch_shapes=[pltpu.VMEM((rows, WIDTH), F32), pltpu.VMEM((rows, 1), F32)],
        ),
        out_shape=jax.ShapeDtypeStruct((b, t_new, WIDTH), BF16),
        compiler_params=_params("parallel", "arbitrary"),
        name="sb_sample",
    )(page_table, qbd, bias_rows, knew, vnew,
      *([cache_k] * pages_per_step), *([cache_v] * pages_per_step))


def _outproj_kernel(x_ref, oa_ref, ob_ref, w_ref, y_ref):
    y_ref[...] = (x_ref[...] + _dot(oa_ref[...], w_ref[0:WIDTH, :])
                  + _dot(ob_ref[...], w_ref[WIDTH:2 * WIDTH, :]))


def _outproj(x, oa, ob, w, tile):
    n, d = x.shape
    return pl.pallas_call(
        _outproj_kernel,
        grid=(n // tile,),
        in_specs=[pl.BlockSpec((tile, d), lambda i: (i, 0)),
                  pl.BlockSpec((tile, WIDTH), lambda i: (i, 0)),
                  pl.BlockSpec((tile, WIDTH), lambda i: (i, 0)),
                  pl.BlockSpec((2 * WIDTH, d), lambda i: (0, 0))],
        out_specs=pl.BlockSpec((tile, d), lambda i: (i, 0)),
        out_shape=jax.ShapeDtypeStruct((n, d), F32),
        compiler_params=_params("parallel"),
        name="outproj",
    )(x, oa, ob, w)


def _top_values(x, count):
    vals = []
    for _ in range(count):
        m = jnp.max(x, axis=0, keepdims=True)
        vals.append(m)
        x = jnp.where(x == m, -jnp.inf, x)
    return jnp.concatenate(vals, axis=0)


def _route_kernel(x_ref, lnw_ref, wq_ref, k1_ref, k2_ref, h_ref, s1_ref, e1_ref, s2_ref,
                  e2_ref, thr_ref):
    x = x_ref[...]
    hn = x * lax.rsqrt(jnp.mean(x * x, axis=-1, keepdims=True) + RMS_EPS) * lnw_ref[...]
    h_ref[...] = hn.astype(BF16)
    half = PEER_KEYS
    for h in range(N_HEADS):
        q1 = _dot(hn, wq_ref[:, 2 * h * half:(2 * h + 1) * half], HIGHEST)
        q2 = _dot(hn, wq_ref[:, (2 * h + 1) * half:(2 * h + 2) * half], HIGHEST)
        s1 = _dot_nt(k1_ref[...], q1, HIGHEST)
        s2 = _dot_nt(k2_ref[...], q2, HIGHEST)
        v1 = _top_values(s1, PEER_TOPK)
        v2 = _top_values(s2, PEER_TOPK)
        cand = jnp.concatenate([v1[i:i + 1, :] + v2 for i in range(PEER_TOPK)], axis=0)
        thr = _top_values(cand, PEER_TOPK)[PEER_TOPK - 1:PEER_TOPK, :]
        top = v1[0:1, :] + v2[0:1, :]
        z = jnp.sum(jnp.where(cand >= thr, jnp.exp(cand - top), 0.0), axis=0, keepdims=True)
        s1_ref[h] = s1
        s2_ref[h] = s2
        e1_ref[h] = jnp.exp(s1 - v1[0:1, :]) / z
        e2_ref[h] = jnp.exp(s2 - v2[0:1, :])
        thr_ref[h:h + 1, :] = thr


def _route(x, lnw, wq, k1, k2, tile):
    n, d = x.shape
    per_head = pl.BlockSpec((N_HEADS, PEER_KEYS, tile), lambda i: (0, 0, i))
    head_shape = jax.ShapeDtypeStruct((N_HEADS, PEER_KEYS, n), F32)
    return pl.pallas_call(
        _route_kernel,
        grid=(n // tile,),
        in_specs=[pl.BlockSpec((tile, d), lambda i: (i, 0)),
                  pl.BlockSpec((1, d), lambda i: (0, 0)),
                  pl.BlockSpec(wq.shape, lambda i: (0, 0)),
                  pl.BlockSpec(k1.shape, lambda i: (0, 0)),
                  pl.BlockSpec(k2.shape, lambda i: (0, 0))],
        out_specs=[pl.BlockSpec((tile, d), lambda i: (i, 0)),
                   per_head, per_head, per_head, per_head,
                   pl.BlockSpec((N_HEADS, tile), lambda i: (0, i))],
        out_shape=[jax.ShapeDtypeStruct((n, d), BF16),
                   head_shape, head_shape, head_shape, head_shape,
                   jax.ShapeDtypeStruct((N_HEADS, n), F32)],
        compiler_params=_params("parallel"),
        name="peer_route",
    )(x, lnw, wq, k1, k2)


def _gelu_tanh(x):
    c = 0.7978845608028654
    return 0.5 * x * (1.0 + jnp.tanh(c * (x + 0.044715 * (x * x * x))))


def _peer_kernel(x_ref, h_ref, u_ref, vt_ref, s1_ref, e1_ref, s2_ref, e2_ref, thr_ref,
                 y_ref, acc_ref, w_ref, *, expert_tile):
    j = pl.program_id(1)
    tile = h_ref.shape[0]

    @pl.when(j == 0)
    def _():
        acc_ref[...] = jnp.zeros_like(acc_ref)

    act = _gelu_tanh(_dot_nt(u_ref[...], h_ref[...]))
    groups = expert_tile // PEER_KEYS
    for a_local in range(groups):
        a = j * groups + a_local
        gate = jnp.zeros((PEER_KEYS, tile), F32)
        for h in range(N_HEADS):
            total = s1_ref[h, pl.ds(a, 1), :] + s2_ref[h]
            picked = total >= thr_ref[h:h + 1, :]
            gate = gate + jnp.where(picked, e1_ref[h, pl.ds(a, 1), :] * e2_ref[h], 0.0)
        rows = slice(a_local * PEER_KEYS, (a_local + 1) * PEER_KEYS)
        w_ref[rows, :] = (gate * act[rows, :]).astype(BF16)
    acc_ref[...] += _dot(vt_ref[...], w_ref[...])

    @pl.when(j == pl.num_programs(1) - 1)
    def _():
        y_ref[...] = x_ref[...] + acc_ref[...].T


def _peer(x, hn, u, vt, s1, e1, s2, e2, thr, tile, expert_tile):
    n, d = x.shape
    n_exp = u.shape[0]
    per_head = pl.BlockSpec((N_HEADS, PEER_KEYS, tile), lambda i, j: (0, 0, i))
    return pl.pallas_call(
        functools.partial(_peer_kernel, expert_tile=expert_tile),
        grid=(n // tile, n_exp // expert_tile),
        in_specs=[pl.BlockSpec((tile, d), lambda i, j: (i, 0)),
                  pl.BlockSpec((tile, d), lambda i, j: (i, 0)),
                  pl.BlockSpec((expert_tile, d), lambda i, j: (j, 0)),
                  pl.BlockSpec((d, expert_tile), lambda i, j: (0, j)),
                  per_head, per_head, per_head, per_head,
                  pl.BlockSpec((N_HEADS, tile), lambda i, j: (0, i))],
        out_specs=pl.BlockSpec((tile, d), lambda i, j: (i, 0)),
        out_shape=jax.ShapeDtypeStruct((n, d), F32),
        scratch_shapes=[pltpu.VMEM((d, tile), F32), pltpu.VMEM((expert_tile, tile), BF16)],
        compiler_params=_params("parallel", "arbitrary"),
        name="peer_dense",
    )(x, hn, u, vt, s1, e1, s2, e2, thr)


def _norm_kernel(x_ref, w_ref, y_ref):
    x = x_ref[...]
    y_ref[...] = x * lax.rsqrt(jnp.mean(x * x, axis=-1, keepdims=True) + RMS_EPS) * w_ref[...]


def _final_norm(x, w, tile):
    n, d = x.shape
    return pl.pallas_call(
        _norm_kernel,
        grid=(n // tile,),
        in_specs=[pl.BlockSpec((tile, d), lambda i: (i, 0)), pl.BlockSpec((1, d), lambda i: (0, 0))],
        out_specs=pl.BlockSpec((tile, d), lambda i: (i, 0)),
        out_shape=jax.ShapeDtypeStruct((n, d), F32),
        compiler_params=_params("parallel"),
        name="final_norm",
    )(x, w)


ROW_TILE = 256
PEER_EXPERT_TILE = 1024
PAGES_PER_STEP = 8


def _heads_major(a, b, t):
    return a.reshape(b, t, N_HEADS, HEAD_DIM).transpose(0, 2, 1, 3)


def kernel(x_prompt, x_sample, cache_sb_k, cache_sb_v, page_table, state_gdn, state_conv,
           meta_tokens, ln1_w, ln2_w, lnf_w, w_in, conv_w, gdn_a_log, gdn_dt_bias, gdn_norm_w,
           sb_logit_bias, w_out, peer_wq, peer_k1, peer_k2, peer_u, peer_v):
    bp, seq, d = x_prompt.shape
    bs, ts, _ = x_sample.shape
    depth = w_in.shape[0]
    n_pool = cache_sb_k.shape[1]
    pad = (-N_META) % SB_BLOCK
    tp = pad + N_META + seq
    np_rows = bp * tp
    ns_rows = bs * ts
    hist = CONV_TAPS - 1

    meta = jnp.broadcast_to(meta_tokens[None], (bp, N_META, d))
    xp = jnp.concatenate([jnp.zeros((bp, pad, d), F32), meta, x_prompt], axis=1)
    x = jnp.concatenate([xp.reshape(np_rows, d), x_sample.reshape(ns_rows, d)], axis=0)

    cache_k = cache_sb_k.reshape(depth, n_pool, PAGE, WIDTH)
    cache_v = cache_sb_v.reshape(depth, n_pool, PAGE, WIDTH)
    zero_state = jnp.zeros((bp, N_HEADS, HEAD_DIM, HEAD_DIM), F32)
    zero_conv = jnp.zeros((bp, hist, QKV_WIDTH), F32)
    eye_h = jnp.eye(N_HEADS, dtype=BF16)

    outs = {k: [] for k in ("kp", "vp", "ks", "vs", "sp", "ss", "cp", "cs")}
    for l in range(depth):
        wl = w_in[l]
        o_gate = QKV_WIDTH
        o_a = o_gate + WIDTH
        o_qb = o_a + 2 * N_HEADS
        w_perm = jnp.concatenate(
            [wl[:, :o_a], wl[:, o_qb:], wl[:, o_a:o_qb],
             jnp.zeros((d, LANES - 2 * N_HEADS), F32)], axis=1).astype(BF16)
        qkv, gate, ab, qb, kb, vb, kb16, vb16 = _inproj(x, ln1_w[l][None], w_perm, ROW_TILE)

        gdn_consts = (conv_w[l], gdn_a_log[l][None], gdn_dt_bias[l][None], gdn_norm_w[l][None])
        split = lambda a, w_: (a[:np_rows].reshape(bp, tp, w_), a[np_rows:].reshape(bs, ts, w_))
        qkv_p, qkv_s = split(qkv, QKV_WIDTH)
        ab_p, ab_s = split(ab, LANES)
        gate_p, gate_s = split(gate, WIDTH)
        oa_p, s_p, c_p = _gdn(qkv_p, ab_p, gate_p, *gdn_consts, zero_state, zero_conv, GDN_CHUNK)
        oa_s, s_s, c_s = _gdn(qkv_s, ab_s, gate_s, *gdn_consts, state_gdn[l], state_conv[l], ts)

        qb_p, qb_s = split(qb, WIDTH)
        kb16_p, kb16_s = split(kb16, WIDTH)
        vb16_p, vb16_s = split(vb16, WIDTH)
        ob_p = _sb_prompt(_heads_major(qb_p, bp, tp), _heads_major(kb16_p, bp, tp),
                          _heads_major(vb16_p, bp, tp), sb_logit_bias[l], pad)
        ob_p = ob_p.transpose(0, 2, 1, 3).reshape(np_rows, WIDTH)

        q_heads = qb_s.reshape(bs, ts, N_HEADS, HEAD_DIM).transpose(0, 2, 1, 3)
        qbd = (q_heads[:, :, :, None, :] * eye_h[None, :, None, :, None]).reshape(
            bs, N_HEADS * ts, WIDTH)
        bias_rows = jnp.broadcast_to(jnp.repeat(sb_logit_bias[l], ts)[:, None], (N_HEADS * ts, LANES))
        grow = lambda a: jnp.pad(a, ((0, 0), (0, PAGE - ts), (0, 0)))
        ob_s = _sb_sample(page_table, qbd, bias_rows, grow(kb16_s), grow(vb16_s), cache_k, cache_v,
                          l, PAGES_PER_STEP).reshape(ns_rows, WIDTH)

        oa = jnp.concatenate([oa_p.reshape(np_rows, WIDTH), oa_s.reshape(ns_rows, WIDTH)], axis=0)
        ob = jnp.concatenate([ob_p, ob_s], axis=0)
        x = _outproj(x, oa, ob, w_out[l].astype(BF16), ROW_TILE)

        hn, s1, e1, s2, e2, thr = _route(x, ln2_w[l][None], peer_wq[l], peer_k1[l], peer_k2[l], ROW_TILE)
        x = _peer(x, hn, peer_u[l].astype(BF16), peer_v[l].astype(BF16).T, s1, e1, s2, e2, thr,
                  ROW_TILE, PEER_EXPERT_TILE)

        kb_p, kb_s = split(kb, WIDTH)
        vb_p, vb_s = split(vb, WIDTH)
        outs["kp"].append(kb_p[:, pad:].reshape(bp, tp - pad, N_HEADS, HEAD_DIM))
        outs["vp"].append(vb_p[:, pad:].reshape(bp, tp - pad, N_HEADS, HEAD_DIM))
        outs["ks"].append(kb_s.reshape(bs, ts, N_HEADS, HEAD_DIM))
        outs["vs"].append(vb_s.reshape(bs, ts, N_HEADS, HEAD_DIM))
        outs["sp"].append(s_p); outs["ss"].append(s_s)
        outs["cp"].append(c_p); outs["cs"].append(c_s)

    y = _final_norm(x, lnf_w[None], ROW_TILE)
    y_prompt = y[:np_rows].reshape(bp, tp, d)[:, pad + N_META:]
    y_sample = y[np_rows:].reshape(bs, ts, d)
    stack = lambda key: jnp.stack(outs[key])
    return (y_prompt, y_sample, stack("kp"), stack("vp"), stack("ks"), stack("vs"),
            stack("sp"), stack("ss"), stack("cp"), stack("cs"))
```

```python
import functools

import jax
import jax.numpy as jnp
from jax import lax
from jax.experimental import pallas as pl
from jax.experimental.pallas import tpu as pltpu

F32 = jnp.float32
BF16 = jnp.bfloat16
HIGHEST = lax.Precision.HIGHEST

HEAD_DIM = 64
N_HEADS = 8
WIDTH = N_HEADS * HEAD_DIM
QKV_WIDTH = 3 * WIDTH
QUAD = 4
QUAD_WIDTH = QUAD * HEAD_DIM
CONV_TAPS = 4
N_META = 16
GDN_CHUNK = 64
SB_BLOCK = 128
PAGE = 128
PEER_KEYS = 128
PEER_TOPK = 16
RMS_EPS = 1e-6
L2_EPS = 1e-6
LOG2E = 1.4426950408889634
LANES = 128
VMEM_LIMIT = 56 * 1024 * 1024


def _dot(a, b, precision=None):
    return jnp.dot(a, b, preferred_element_type=F32, precision=precision)


def _dot_nt(a, b, precision=None):
    return lax.dot_general(a, b, (((1,), (1,)), ((), ())),
                           preferred_element_type=F32, precision=precision)


def _mm(a, b):
    return _dot(a.astype(BF16), b.astype(BF16))


def _mm_nt(a, b):
    return _dot_nt(a.astype(BF16), b.astype(BF16))


def _split3(x):
    hi = x.astype(BF16)
    rest = x - hi.astype(F32)
    mid = rest.astype(BF16)
    lo = (rest - mid.astype(F32)).astype(BF16)
    return hi, mid, lo


def _params(*sem):
    return pltpu.CompilerParams(dimension_semantics=sem, vmem_limit_bytes=VMEM_LIMIT)


def _iota(shape, axis):
    return lax.broadcasted_iota(jnp.int32, shape, axis)


def _log2_int(n):
    assert n & (n - 1) == 0, n
    return n.bit_length() - 1


def _inproj_kernel(x_ref, lnw_ref, w_ref, qkv_ref, gate_ref, ab_ref, qb_ref,
                   kb_ref, vb_ref, kb16_ref, vb16_ref):
    x = x_ref[...]
    h = x * lax.rsqrt(jnp.mean(x * x, axis=-1, keepdims=True) + RMS_EPS) * lnw_ref[...]
    h = h.astype(BF16)
    o = 0
    qkv_ref[...] = _dot(h, w_ref[:, o:o + QKV_WIDTH]); o += QKV_WIDTH
    gate_ref[...] = _dot(h, w_ref[:, o:o + WIDTH]); o += WIDTH
    qb_ref[...] = (_dot(h, w_ref[:, o:o + WIDTH]) * (LOG2E * HEAD_DIM ** -0.5)).astype(BF16); o += WIDTH
    kb = _dot(h, w_ref[:, o:o + WIDTH]); o += WIDTH
    kb_ref[...] = kb
    kb16_ref[...] = kb.astype(BF16)
    vb = _dot(h, w_ref[:, o:o + WIDTH]); o += WIDTH
    vb_ref[...] = vb
    vb16_ref[...] = vb.astype(BF16)
    ab_ref[...] = _dot(h, w_ref[:, o:o + LANES])


def _inproj(x, lnw, w, tile):
    n, d = x.shape
    wcols = w.shape[1]
    row = lambda width: pl.BlockSpec((tile, width), lambda i: (i, 0))
    shp = lambda width, dt: jax.ShapeDtypeStruct((n, width), dt)
    return pl.pallas_call(
        _inproj_kernel,
        grid=(n // tile,),
        in_specs=[row(d), pl.BlockSpec((1, d), lambda i: (0, 0)),
                  pl.BlockSpec((d, wcols), lambda i: (0, 0))],
        out_specs=[row(QKV_WIDTH), row(WIDTH), row(LANES), row(WIDTH), row(WIDTH),
                   row(WIDTH), row(WIDTH), row(WIDTH)],
        out_shape=[shp(QKV_WIDTH, F32), shp(WIDTH, F32), shp(LANES, F32), shp(WIDTH, BF16),
                   shp(WIDTH, F32), shp(WIDTH, F32), shp(WIDTH, BF16), shp(WIDTH, BF16)],
        compiler_params=_params("parallel"),
        name="inproj",
    )(x, lnw, w)


def _silu(x):
    return x * (1.0 / (1.0 + jnp.exp(-x)))


def _softplus(x):
    return jnp.maximum(x, 0.0) + jnp.log1p(jnp.exp(-jnp.abs(x)))


def _dot_sel(x, sel):
    hi, mid, lo = _split3(x)
    return _dot(hi, sel) + _dot(mid, sel) + _dot(lo, sel)


def _unit_lower_inverse(low, rows, block):
    ri = _iota((rows, rows), 0)
    ci = _iota((rows, rows), 1)
    inv = (ri == ci).astype(F32)
    s = 1
    while s < block:
        off = ((ri & s) != 0) & ((ci & s) == 0) & ((ri ^ ci) < 2 * s)
        blk = jnp.where(off, low, 0.0)
        inv = inv - _mm(inv, _mm(blk, inv))
        s *= 2
    return inv


def _gdn_kernel(qkv_ref, ab_ref, gate_ref, convw_ref, alog_ref, dtb_ref, normw_ref,
                s0_ref, conv0_ref, o_ref, sout_ref, convout_ref, xbuf, s_scr, *, chunk):
    c = pl.program_id(1)
    hist = CONV_TAPS - 1
    rows = QUAD * chunk
    chunk_shift = _log2_int(chunk)
    head_shift = _log2_int(HEAD_DIM)
    n_quads = N_HEADS // QUAD

    @pl.when(c == 0)
    def _():
        xbuf[0:8, :] = jnp.zeros((8, QKV_WIDTH), F32)
        xbuf[8 - hist:8, :] = conv0_ref[0]
        s_scr[...] = jnp.zeros_like(s_scr)
        for h in range(N_HEADS):
            blk = slice((h % QUAD) * HEAD_DIM, (h % QUAD + 1) * HEAD_DIM)
            s_scr[h // QUAD, blk, blk] = s0_ref[0, h]

    xbuf[8:8 + chunk, :] = qkv_ref[0]
    conv = convw_ref[0:1, :] * xbuf[8 - hist:8 - hist + chunk, :]
    for i in range(1, CONV_TAPS):
        conv = conv + convw_ref[i:i + 1, :] * xbuf[8 - hist + i:8 - hist + i + chunk, :]
    convout_ref[0] = xbuf[8 + chunk - hist:8 + chunk, :]
    tail = xbuf[chunk:chunk + 8, :]
    xbuf[0:8, :] = tail
    act = _silu(conv)

    ab = ab_ref[0]
    lane = _iota((chunk, LANES), 1)
    g = -jnp.exp(alog_ref[...]) * _softplus(ab + dtb_ref[...])
    beta = 1.0 / (1.0 + jnp.exp(-ab))
    tril = (_iota((chunk, chunk), 0) >= _iota((chunk, chunk), 1)).astype(BF16)
    g_hi, g_mid, g_lo = _split3(g)
    gcum = _dot(tril, g_hi) + _dot(tril, g_mid) + _dot(tril, g_lo)
    eg_beta = jnp.where(lane < N_HEADS, jnp.exp(gcum), beta)
    edec = jnp.exp(gcum[chunk - 1:chunk, :] - gcum)

    lane_head = lax.shift_right_logical(_iota((LANES, QUAD_WIDTH), 1), head_shift)
    src_lane = _iota((LANES, QUAD_WIDTH), 0)
    block_ones = (lax.shift_right_logical(_iota((QUAD_WIDTH, QUAD_WIDTH), 0), head_shift)
                  == lax.shift_right_logical(_iota((QUAD_WIDTH, QUAD_WIDTH), 1), head_shift)).astype(BF16)
    ones_k = jnp.ones((LANES, rows), BF16)
    ri = _iota((rows, rows), 0)
    ci = _iota((rows, rows), 1)
    same = lax.shift_right_logical(ri, chunk_shift) == lax.shift_right_logical(ci, chunk_shift)
    incl = same & (ri >= ci)
    strict = same & (ri > ci)
    exp_row_head = lax.shift_right_logical(_iota((rows, QUAD_WIDTH), 0), chunk_shift)
    exp_lane_head = lax.shift_right_logical(_iota((rows, QUAD_WIDTH), 1), head_shift)
    own_block = exp_row_head == exp_lane_head
    stack_row_head = lax.shift_right_logical(_iota((rows, LANES), 0), chunk_shift)
    stack_lane = _iota((rows, LANES), 1)

    def expand(x):
        return jnp.where(own_block, jnp.concatenate([x] * QUAD, axis=0), 0.0)

    for quad in range(n_quads):
        first = quad * QUAD
        sel_g = (src_lane == lane_head + first).astype(BF16)
        sel_b = (src_lane == lane_head + (N_HEADS + first)).astype(BF16)
        eg_w = _dot_sel(eg_beta, sel_g)
        beta_w = _dot_sel(eg_beta, sel_b)
        edec_w = _dot_sel(edec, sel_g)

        cols = slice(quad * QUAD_WIDTH, (quad + 1) * QUAD_WIDTH)
        q = act[:, cols]
        k = act[:, WIDTH + quad * QUAD_WIDTH:WIDTH + (quad + 1) * QUAD_WIDTH]
        v = act[:, 2 * WIDTH + quad * QUAD_WIDTH:2 * WIDTH + (quad + 1) * QUAD_WIDTH]
        q = q * lax.rsqrt(_dot_sel(q * q, block_ones) + L2_EPS) * (HEAD_DIM ** -0.5)
        k = k * lax.rsqrt(_dot_sel(k * k, block_ones) + L2_EPS)
        kbeta = k * beta_w

        stacked = jnp.where(stack_lane == stack_row_head + first,
                            jnp.concatenate([gcum] * QUAD, axis=0), 0.0)
        s_hi, s_mid, s_lo = _split3(stacked)
        g_row = _dot(s_hi, ones_k) + _dot(s_mid, ones_k) + _dot(s_lo, ones_k)
        ones_r = jnp.ones((rows, LANES), BF16)
        g_col = _dot_nt(ones_r, s_hi) + _dot_nt(ones_r, s_mid) + _dot_nt(ones_r, s_lo)
        decay = jnp.where(incl, jnp.exp(g_row - g_col), 0.0)

        e_k = expand(k).astype(BF16)
        low = jnp.where(strict, _dot_nt(expand(kbeta).astype(BF16), e_k) * decay, 0.0)
        inv = _unit_lower_inverse(low, rows, chunk)
        state = s_scr[quad]
        sol_v = _mm(inv, expand(v * beta_w))
        sol_k = _mm(inv, expand(kbeta * eg_w))
        u = sol_v - _mm(sol_k, state)
        attn = _dot_nt(expand(q).astype(BF16), e_k) * decay
        e_o = _mm(expand(q * eg_w), state) + _mm(attn, u)
        s_scr[quad] = state * eg_w[chunk - 1:chunk, :] + lax.dot_general(
            expand(k * edec_w), u, (((0,), (0,)), ((), ())), preferred_element_type=F32)
        o = e_o[0:chunk, :]
        for h in range(1, QUAD):
            o = o + e_o[h * chunk:(h + 1) * chunk, :]
        on = o * lax.rsqrt(_dot_sel(o * o, block_ones) * (1.0 / HEAD_DIM) + RMS_EPS)
        o_ref[0, :, cols] = (on * normw_ref[...] * _silu(gate_ref[0, :, cols])).astype(o_ref.dtype)

    @pl.when(c == pl.num_programs(1) - 1)
    def _():
        for h in range(N_HEADS):
            blk = slice((h % QUAD) * HEAD_DIM, (h % QUAD + 1) * HEAD_DIM)
            sout_ref[0, h] = s_scr[h // QUAD, blk, blk]


def _gdn(qkv, ab, gate, convw, alog, dtb, normw, s0, conv0, chunk):
    b, t, _ = qkv.shape
    n = t // chunk
    tok = lambda width: pl.BlockSpec((1, chunk, width), lambda i, c: (i, c, 0))
    const2 = lambda r, cdim: pl.BlockSpec((r, cdim), lambda i, c: (0, 0))
    hist = CONV_TAPS - 1
    state_spec = pl.BlockSpec((1, N_HEADS, HEAD_DIM, HEAD_DIM), lambda i, c: (i, 0, 0, 0))
    conv_spec = pl.BlockSpec((1, hist, QKV_WIDTH), lambda i, c: (i, 0, 0))
    return pl.pallas_call(
        functools.partial(_gdn_kernel, chunk=chunk),
        grid=(b, n),
        in_specs=[tok(QKV_WIDTH), tok(LANES), tok(WIDTH),
                  const2(CONV_TAPS, QKV_WIDTH), const2(1, LANES), const2(1, LANES),
                  const2(1, QUAD_WIDTH), state_spec, conv_spec],
        out_specs=[tok(WIDTH), state_spec, conv_spec],
        out_shape=[jax.ShapeDtypeStruct((b, t, WIDTH), BF16),
                   jax.ShapeDtypeStruct((b, N_HEADS, HEAD_DIM, HEAD_DIM), F32),
                   jax.ShapeDtypeStruct((b, hist, QKV_WIDTH), F32)],
        scratch_shapes=[pltpu.VMEM((chunk + 8, QKV_WIDTH), F32),
                        pltpu.VMEM((N_HEADS // QUAD, QUAD_WIDTH, QUAD_WIDTH), F32)],
        compiler_params=_params("parallel", "arbitrary"),
        name=f"gdn_c{chunk}",
    )(qkv, ab, gate, convw, alog, dtb, normw, s0, conv0)


def _suffix_total():
    r = _iota((SB_BLOCK, 2 * SB_BLOCK), 0)
    c = _iota((SB_BLOCK, 2 * SB_BLOCK), 1)
    return ((r > c) | (c >= SB_BLOCK)).astype(BF16)


def _sb_block(z, mask, run, suffix_total):
    sp = jnp.log2(1.0 + jnp.exp2(-jnp.abs(z)))
    log_beta = jnp.minimum(z, 0.0) - sp
    log_stay = log_beta - z
    if mask is not None:
        log_stay = jnp.where(mask, log_stay, 0.0)
    sums = _dot(log_stay.astype(BF16), suffix_total)
    w = jnp.exp2(log_beta + run + sums[:, :SB_BLOCK])
    if mask is not None:
        w = jnp.where(mask, w, 0.0)
    return w, run + sums[:, SB_BLOCK:]


def _sb_prompt_kernel(bias_ref, q_ref, k_ref, v_ref, o_ref, qm_ref, acc_ref, run_ref, *, pad):
    i = pl.program_id(1)
    pair_lanes = 2 * HEAD_DIM
    n_pairs = N_HEADS // 2
    ri = _iota((SB_BLOCK, SB_BLOCK), 0)
    ci = _iota((SB_BLOCK, SB_BLOCK), 1)
    suffix_total = _suffix_total()

    for p in range(n_pairs):
        qp = q_ref[0, :, p * pair_lanes:(p + 1) * pair_lanes]
        qm_ref[2 * p] = jnp.where(ci < HEAD_DIM, qp, jnp.zeros_like(qp))
        qm_ref[2 * p + 1] = jnp.where(ci >= HEAD_DIM, qp, jnp.zeros_like(qp))
    acc_ref[...] = jnp.zeros_like(acc_ref)
    run_ref[...] = jnp.zeros_like(run_ref)

    def visit(j, mask):
        start = pl.multiple_of(j * SB_BLOCK, SB_BLOCK)
        kv = [(k_ref[0, pl.ds(start, SB_BLOCK), p * pair_lanes:(p + 1) * pair_lanes],
               v_ref[0, pl.ds(start, SB_BLOCK), p * pair_lanes:(p + 1) * pair_lanes])
              for p in range(n_pairs)]
        runs = [run_ref[h] for h in range(N_HEADS)]
        zs = [_dot_nt(qm_ref[h], kv[h // 2][0]) + bias_ref[h] * LOG2E for h in range(N_HEADS)]
        blocks = [_sb_block(zs[h], mask, runs[h], suffix_total) for h in range(N_HEADS)]
        outs = [_dot(blocks[h][0].astype(BF16), kv[h // 2][1]) for h in range(N_HEADS)]
        for h in range(N_HEADS):
            run_ref[h] = blocks[h][1]
            acc_ref[h] += outs[h]

    visit(i, (ci < ri) & (ci + i * SB_BLOCK >= pad))

    def middle(step, carry):
        visit(i - step, None)
        return carry

    lax.fori_loop(1, i, middle, 0)

    @pl.when(i > 0)
    def _():
        visit(0, ci >= pad)

    for p in range(n_pairs):
        o_ref[0, :, p * pair_lanes:(p + 1) * pair_lanes] = jnp.where(
            ci < HEAD_DIM, acc_ref[2 * p], acc_ref[2 * p + 1]).astype(o_ref.dtype)


def _sb_prompt(q, k, v, bias, pad):
    b, t, width = q.shape
    nq = t // SB_BLOCK
    resident = pl.BlockSpec((1, t, width), lambda bi, i: (bi, 0, 0), pipeline_mode=pl.Buffered(1))
    return pl.pallas_call(
        functools.partial(_sb_prompt_kernel, pad=pad),
        grid=(b, nq),
        in_specs=[pl.BlockSpec(memory_space=pltpu.SMEM),
                  pl.BlockSpec((1, SB_BLOCK, width), lambda bi, i: (bi, i, 0)),
                  resident, resident],
        out_specs=pl.BlockSpec((1, SB_BLOCK, width), lambda bi, i: (bi, i, 0)),
        out_shape=jax.ShapeDtypeStruct((b, t, width), BF16),
        scratch_shapes=[pltpu.VMEM((N_HEADS, SB_BLOCK, SB_BLOCK), BF16),
                        pltpu.VMEM((N_HEADS, SB_BLOCK, SB_BLOCK), F32),
                        pltpu.VMEM((N_HEADS, SB_BLOCK, SB_BLOCK), F32)],
        compiler_params=_params("parallel", "arbitrary"),
        name="sb_prompt",
    )(bias, q, k, v)


def _sb_sample_kernel(pt_ref, q_ref, bias_ref, knew_ref, vnew_ref, *refs, pages_per_step, t_new):
    k_refs = refs[:pages_per_step]
    v_refs = refs[pages_per_step:2 * pages_per_step]
    o_ref = refs[2 * pages_per_step]
    acc_ref, run_ref = refs[2 * pages_per_step + 1:]
    g = pl.program_id(1)
    rows = N_HEADS * t_new
    bias = bias_ref[...] * LOG2E
    suffix_total = _suffix_total()

    def visit(state, k_of_head, v_of_head, mask):
        run, accs = state
        z = jnp.concatenate([_dot_nt(q_ref[0, h], k_of_head(h)) for h in range(N_HEADS)], axis=0) + bias
        w, run = _sb_block(z, mask, run, suffix_total)
        return run, [accs[h] + _dot(w[h * t_new:(h + 1) * t_new, :].astype(BF16), v_of_head(h))
                     for h in range(N_HEADS)]

    def store_state(state):
        run_ref[...] = state[0]
        for h in range(N_HEADS):
            acc_ref[h] = state[1][h]

    @pl.when(g == 0)
    def _():
        ri = _iota((rows, LANES), 0)
        ci = _iota((rows, LANES), 1)
        zero = (jnp.zeros((rows, LANES), F32), [jnp.zeros((t_new, HEAD_DIM), F32)] * N_HEADS)
        store_state(visit(zero, lambda h: knew_ref[0, h], lambda h: vnew_ref[0, h],
                          ci < lax.rem(ri, t_new)))

    def head_rows(ref, h):
        return ref[0, 0, pl.ds(h, PAGE, stride=N_HEADS), :].astype(BF16)

    state = (run_ref[...], [acc_ref[h] for h in range(N_HEADS)])
    for p in reversed(range(pages_per_step)):
        state = visit(state, functools.partial(head_rows, k_refs[p]),
                      functools.partial(head_rows, v_refs[p]), None)
    store_state(state)

    @pl.when(g == pl.num_programs(1) - 1)
    def _():
        for h in range(N_HEADS):
            o_ref[0, :, h * HEAD_DIM:(h + 1) * HEAD_DIM] = acc_ref[h].astype(o_ref.dtype)


def _sb_sample(page_table, q, bias_rows, knew, vnew, cache_k, cache_v, layer, pages_per_step):
    b, _, t_new, _ = q.shape
    rows = N_HEADS * t_new
    n_pages = page_table.shape[1]
    n_groups = n_pages // pages_per_step

    def page_spec(p):
        def index(bi, g, pt):
            return (layer, pt[bi, (n_groups - 1 - g) * pages_per_step + p], 0, 0)
        return pl.BlockSpec((1, 1, PAGE * N_HEADS, HEAD_DIM), index)

    per_batch = lambda r: pl.BlockSpec((1, N_HEADS, r, HEAD_DIM), lambda bi, g, pt: (bi, 0, 0, 0))
    return pl.pallas_call(
        functools.partial(_sb_sample_kernel, pages_per_step=pages_per_step, t_new=t_new),
        grid_spec=pltpu.PrefetchScalarGridSpec(
            num_scalar_prefetch=1,
            grid=(b, n_groups),
            in_specs=[per_batch(t_new),
                      pl.BlockSpec((rows, LANES), lambda bi, g, pt: (0, 0)),
                      per_batch(PAGE), per_batch(PAGE)]
                     + [page_spec(p) for p in range(pages_per_step)] * 2,
            out_specs=pl.BlockSpec((1, t_new, WIDTH), lambda bi, g, pt: (bi, 0, 0)),
            scratch_shapes=[pltpu.VMEM((N_HEADS, t_new, HEAD_DIM), F32),
                            pltpu.VMEM((rows, LANES), F32)],
        ),
        out_shape=jax.ShapeDtypeStruct((b, t_new, WIDTH), BF16),
        compiler_params=_params("parallel", "arbitrary"),
        name="sb_sample",
    )(page_table, q, bias_rows, knew, vnew,
      *([cache_k] * pages_per_step), *([cache_v] * pages_per_step))


def _outproj_kernel(x_ref, oa_ref, ob_ref, w_ref, y_ref):
    y_ref[...] = (x_ref[...] + _dot(oa_ref[...], w_ref[0:WIDTH, :])
                  + _dot(ob_ref[...], w_ref[WIDTH:2 * WIDTH, :]))


def _outproj(x, oa, ob, w, tile):
    n, d = x.shape
    return pl.pallas_call(
        _outproj_kernel,
        grid=(n // tile,),
        in_specs=[pl.BlockSpec((tile, d), lambda i: (i, 0)),
                  pl.BlockSpec((tile, WIDTH), lambda i: (i, 0)),
                  pl.BlockSpec((tile, WIDTH), lambda i: (i, 0)),
                  pl.BlockSpec((2 * WIDTH, d), lambda i: (0, 0))],
        out_specs=pl.BlockSpec((tile, d), lambda i: (i, 0)),
        out_shape=jax.ShapeDtypeStruct((n, d), F32),
        compiler_params=_params("parallel"),
        name="outproj",
    )(x, oa, ob, w)


def _top_values(x, count):
    vals = []
    for _ in range(count):
        m = jnp.max(x, axis=0, keepdims=True)
        vals.append(m)
        x = jnp.where(x == m, -jnp.inf, x)
    return jnp.concatenate(vals, axis=0)


def _route_kernel(x_ref, lnw_ref, wq_ref, k1_ref, k2_ref, h_ref, e1_ref, thr_ref, e2_ref):
    x = x_ref[...]
    hn = x * lax.rsqrt(jnp.mean(x * x, axis=-1, keepdims=True) + RMS_EPS) * lnw_ref[...]
    h_ref[...] = hn.astype(BF16)
    half = PEER_KEYS
    for h in range(N_HEADS):
        q1 = _dot(hn, wq_ref[:, 2 * h * half:(2 * h + 1) * half], HIGHEST)
        q2 = _dot(hn, wq_ref[:, (2 * h + 1) * half:(2 * h + 2) * half], HIGHEST)
        s1 = _dot_nt(k1_ref[...], q1, HIGHEST)
        s2 = _dot_nt(k2_ref[...], q2, HIGHEST)
        v1 = _top_values(s1, PEER_TOPK)
        v2 = _top_values(s2, PEER_TOPK)
        cand = [v1[i:i + 1, :] + v2 for i in range(PEER_TOPK)]
        kth = _top_values(jnp.concatenate(cand, axis=0), PEER_TOPK)[PEER_TOPK - 1:PEER_TOPK, :]
        top = v1[0:1, :] + v2[0:1, :]
        e2_top = jnp.exp(v2 - v2[0:1, :])
        z = jnp.zeros_like(top)
        thr = jnp.full_like(s1, jnp.inf)
        for i in range(PEER_TOPK):
            picked = cand[i] >= kth
            z = z + jnp.sum(jnp.where(picked, jnp.exp(cand[i] - top), 0.0), axis=0, keepdims=True)
            row_thr = jnp.min(jnp.where(picked, e2_top, jnp.inf), axis=0, keepdims=True)
            thr = jnp.where(s1 == v1[i:i + 1, :], row_thr, thr)
        e1_ref[h] = jnp.exp(s1 - v1[0:1, :]) / z
        thr_ref[h] = thr
        e2_ref[h] = jnp.exp(s2 - v2[0:1, :])


def _route(x, lnw, wq, k1, k2, tile):
    n, d = x.shape
    per_head = pl.BlockSpec((N_HEADS, PEER_KEYS, tile), lambda i: (0, 0, i))
    head_shape = jax.ShapeDtypeStruct((N_HEADS, PEER_KEYS, n), F32)
    return pl.pallas_call(
        _route_kernel,
        grid=(n // tile,),
        in_specs=[pl.BlockSpec((tile, d), lambda i: (i, 0)),
                  pl.BlockSpec((1, d), lambda i: (0, 0)),
                  pl.BlockSpec(wq.shape, lambda i: (0, 0)),
                  pl.BlockSpec(k1.shape, lambda i: (0, 0)),
                  pl.BlockSpec(k2.shape, lambda i: (0, 0))],
        out_specs=[pl.BlockSpec((tile, d), lambda i: (i, 0)), per_head, per_head, per_head],
        out_shape=[jax.ShapeDtypeStruct((n, d), BF16), head_shape, head_shape, head_shape],
        compiler_params=_params("parallel"),
        name="peer_route",
    )(x, lnw, wq, k1, k2)


def _gelu_tanh(x):
    c = 0.7978845608028654
    return 0.5 * x * (1.0 + jnp.tanh(c * (x + 0.044715 * (x * x * x))))


def _peer_kernel(x_ref, h_ref, u_ref, vt_ref, e1_ref, thr_ref, e2_ref, y_ref, acc_ref, w_ref,
                 *, expert_tile):
    j = pl.program_id(1)
    tile = h_ref.shape[0]
    groups = expert_tile // PEER_KEYS

    @pl.when(j == 0)
    def _():
        acc_ref[...] = jnp.zeros_like(acc_ref)

    act = _gelu_tanh(_dot_nt(u_ref[...], h_ref[...]))
    for a_local in range(groups):
        a = j * groups + a_local
        gate = jnp.zeros((PEER_KEYS, tile), F32)
        for h in range(N_HEADS):
            e2 = e2_ref[h]
            gate = gate + jnp.where(e2 >= thr_ref[h, pl.ds(a, 1), :], e2, 0.0) * e1_ref[h, pl.ds(a, 1), :]
        rows = slice(a_local * PEER_KEYS, (a_local + 1) * PEER_KEYS)
        w_ref[rows, :] = (gate * act[rows, :]).astype(BF16)
    acc_ref[...] += _dot(vt_ref[...], w_ref[...])

    @pl.when(j == pl.num_programs(1) - 1)
    def _():
        y_ref[...] = x_ref[...] + acc_ref[...].T


def _peer(x, hn, u, vt, e1, thr, e2, tile, expert_tile):
    n, d = x.shape
    per_head = pl.BlockSpec((N_HEADS, PEER_KEYS, tile), lambda i, j: (0, 0, i))
    return pl.pallas_call(
        functools.partial(_peer_kernel, expert_tile=expert_tile),
        grid=(n // tile, u.shape[0] // expert_tile),
        in_specs=[pl.BlockSpec((tile, d), lambda i, j: (i, 0)),
                  pl.BlockSpec((tile, d), lambda i, j: (i, 0)),
                  pl.BlockSpec((expert_tile, d), lambda i, j: (j, 0)),
                  pl.BlockSpec((d, expert_tile), lambda i, j: (0, j)),
                  per_head, per_head, per_head],
        out_specs=pl.BlockSpec((tile, d), lambda i, j: (i, 0)),
        out_shape=jax.ShapeDtypeStruct((n, d), F32),
        scratch_shapes=[pltpu.VMEM((d, tile), F32), pltpu.VMEM((expert_tile, tile), BF16)],
        compiler_params=_params("parallel", "arbitrary"),
        name="peer_dense",
    )(x, hn, u, vt, e1, thr, e2)


def _norm_kernel(x_ref, w_ref, y_ref):
    x = x_ref[...]
    y_ref[...] = x * lax.rsqrt(jnp.mean(x * x, axis=-1, keepdims=True) + RMS_EPS) * w_ref[...]


def _final_norm(x, w, tile):
    n, d = x.shape
    return pl.pallas_call(
        _norm_kernel,
        grid=(n // tile,),
        in_specs=[pl.BlockSpec((tile, d), lambda i: (i, 0)), pl.BlockSpec((1, d), lambda i: (0, 0))],
        out_specs=pl.BlockSpec((tile, d), lambda i: (i, 0)),
        out_shape=jax.ShapeDtypeStruct((n, d), F32),
        compiler_params=_params("parallel"),
        name="final_norm",
    )(x, w)


ROW_TILE = 256
PEER_EXPERT_TILE = 1024
PAGES_PER_STEP = 8


def _heads_major(a):
    b, t, _ = a.shape
    return a.reshape(b, t, N_HEADS, HEAD_DIM).transpose(0, 2, 1, 3)


def _pad_lanes(a, width):
    return jnp.pad(a, (0, width - a.shape[0]))[None]


def kernel(x_prompt, x_sample, cache_sb_k, cache_sb_v, page_table, state_gdn, state_conv,
           meta_tokens, ln1_w, ln2_w, lnf_w, w_in, conv_w, gdn_a_log, gdn_dt_bias, gdn_norm_w,
           sb_logit_bias, w_out, peer_wq, peer_k1, peer_k2, peer_u, peer_v):
    bp, seq, d = x_prompt.shape
    bs, ts, _ = x_sample.shape
    depth = w_in.shape[0]
    pad = (-N_META) % SB_BLOCK
    tp = pad + N_META + seq
    np_rows = bp * tp
    ns_rows = bs * ts
    hist = CONV_TAPS - 1

    meta = jnp.broadcast_to(meta_tokens[None], (bp, N_META, d))
    xp = jnp.concatenate([jnp.zeros((bp, pad, d), F32), meta, x_prompt], axis=1)
    x = jnp.concatenate([xp.reshape(np_rows, d), x_sample.reshape(ns_rows, d)], axis=0)

    n_pool = cache_sb_k.shape[1]
    cache_k = cache_sb_k.reshape(depth, n_pool, PAGE * N_HEADS, HEAD_DIM)
    cache_v = cache_sb_v.reshape(depth, n_pool, PAGE * N_HEADS, HEAD_DIM)
    zero_state = jnp.zeros((bp, N_HEADS, HEAD_DIM, HEAD_DIM), F32)
    zero_conv = jnp.zeros((bp, hist, QKV_WIDTH), F32)

    outs = {k: [] for k in ("kp", "vp", "ks", "vs", "sp", "ss", "cp", "cs")}
    for l in range(depth):
        wl = w_in[l]
        o_gate = QKV_WIDTH
        o_a = o_gate + WIDTH
        o_qb = o_a + 2 * N_HEADS
        w_perm = jnp.concatenate(
            [wl[:, :o_a], wl[:, o_qb:], wl[:, o_a:o_qb],
             jnp.zeros((d, LANES - 2 * N_HEADS), F32)], axis=1).astype(BF16)
        qkv, gate, ab, qb, kb, vb, kb16, vb16 = _inproj(x, ln1_w[l][None], w_perm, ROW_TILE)

        gdn_consts = (conv_w[l], _pad_lanes(gdn_a_log[l], LANES), _pad_lanes(gdn_dt_bias[l], LANES),
                      jnp.tile(gdn_norm_w[l], QUAD)[None])
        split = lambda a, w_: (a[:np_rows].reshape(bp, tp, w_), a[np_rows:].reshape(bs, ts, w_))
        qkv_p, qkv_s = split(qkv, QKV_WIDTH)
        ab_p, ab_s = split(ab, LANES)
        gate_p, gate_s = split(gate, WIDTH)
        oa_p, s_p, c_p = _gdn(qkv_p, ab_p, gate_p, *gdn_consts, zero_state, zero_conv, GDN_CHUNK)
        oa_s, s_s, c_s = _gdn(qkv_s, ab_s, gate_s, *gdn_consts, state_gdn[l], state_conv[l], ts)

        qb_p, qb_s = split(qb, WIDTH)
        kb16_p, kb16_s = split(kb16, WIDTH)
        vb16_p, vb16_s = split(vb16, WIDTH)
        ob_p = _sb_prompt(qb_p, kb16_p, vb16_p, sb_logit_bias[l], pad).reshape(np_rows, WIDTH)

        bias_rows = jnp.broadcast_to(jnp.repeat(sb_logit_bias[l], ts)[:, None], (N_HEADS * ts, LANES))
        grow = lambda a: jnp.pad(_heads_major(a), ((0, 0), (0, 0), (0, PAGE - ts), (0, 0)))
        ob_s = _sb_sample(page_table, _heads_major(qb_s), bias_rows, grow(kb16_s), grow(vb16_s),
                          cache_k, cache_v, l, PAGES_PER_STEP).reshape(ns_rows, WIDTH)

        oa = jnp.concatenate([oa_p.reshape(np_rows, WIDTH), oa_s.reshape(ns_rows, WIDTH)], axis=0)
        ob = jnp.concatenate([ob_p, ob_s], axis=0)
        x = _outproj(x, oa, ob, w_out[l].astype(BF16), ROW_TILE)

        hn, e1, thr, e2 = _route(x, ln2_w[l][None], peer_wq[l], peer_k1[l], peer_k2[l], ROW_TILE)
        x = _peer(x, hn, peer_u[l].astype(BF16), peer_v[l].astype(BF16).T, e1, thr, e2,
                  ROW_TILE, PEER_EXPERT_TILE)

        kb_p, kb_s = split(kb, WIDTH)
        vb_p, vb_s = split(vb, WIDTH)
        outs["kp"].append(kb_p[:, pad:].reshape(bp, tp - pad, N_HEADS, HEAD_DIM))
        outs["vp"].append(vb_p[:, pad:].reshape(bp, tp - pad, N_HEADS, HEAD_DIM))
        outs["ks"].append(kb_s.reshape(bs, ts, N_HEADS, HEAD_DIM))
        outs["vs"].append(vb_s.reshape(bs, ts, N_HEADS, HEAD_DIM))
        outs["sp"].append(s_p); outs["ss"].append(s_s)
        outs["cp"].append(c_p); outs["cs"].append(c_s)

    y = _final_norm(x, lnf_w[None], ROW_TILE)
    y_prompt = y[:np_rows].reshape(bp, tp, d)[:, pad + N_META:]
    y_sample = y[np_rows:].reshape(bs, ts, d)
    stack = lambda key: jnp.stack(outs[key])
    return (y_prompt, y_sample, stack("kp"), stack("vp"), stack("ks"), stack("vs"),
            stack("sp"), stack("ss"), stack("cp"), stack("cs"))
```

```python
import functools

import jax
import jax.numpy as jnp
from jax import lax
from jax.experimental import pallas as pl
from jax.experimental.pallas import tpu as pltpu

F32 = jnp.float32
BF16 = jnp.bfloat16
HIGHEST = lax.Precision.HIGHEST

HEAD_DIM = 64
N_HEADS = 8
WIDTH = N_HEADS * HEAD_DIM
QKV_WIDTH = 3 * WIDTH
QUAD = 4
QUAD_WIDTH = QUAD * HEAD_DIM
CONV_TAPS = 4
N_META = 16
GDN_CHUNK = 64
SB_BLOCK = 128
PAGE = 128
PEER_KEYS = 128
PEER_TOPK = 16
RMS_EPS = 1e-6
L2_EPS = 1e-6
LOG2E = 1.4426950408889634
LANES = 128
VMEM_LIMIT = 56 * 1024 * 1024


def _dot(a, b, precision=None):
    return jnp.dot(a, b, preferred_element_type=F32, precision=precision)


def _dot_nt(a, b, precision=None):
    return lax.dot_general(a, b, (((1,), (1,)), ((), ())),
                           preferred_element_type=F32, precision=precision)


def _mm(a, b):
    return _dot(a.astype(BF16), b.astype(BF16))


def _mm_nt(a, b):
    return _dot_nt(a.astype(BF16), b.astype(BF16))


def _split3(x):
    hi = x.astype(BF16)
    rest = x - hi.astype(F32)
    mid = rest.astype(BF16)
    lo = (rest - mid.astype(F32)).astype(BF16)
    return hi, mid, lo


def _params(*sem):
    return pltpu.CompilerParams(dimension_semantics=sem, vmem_limit_bytes=VMEM_LIMIT)


def _iota(shape, axis):
    return lax.broadcasted_iota(jnp.int32, shape, axis)


def _log2_int(n):
    assert n & (n - 1) == 0, n
    return n.bit_length() - 1


def _inproj_kernel(x_ref, lnw_ref, w_ref, qkv_ref, gate_ref, ab_ref, qb_ref,
                   kb_ref, vb_ref, kb16_ref, vb16_ref):
    x = x_ref[...]
    h = x * lax.rsqrt(jnp.mean(x * x, axis=-1, keepdims=True) + RMS_EPS) * lnw_ref[...]
    h = h.astype(BF16)
    o = 0
    qkv_ref[...] = _dot(h, w_ref[:, o:o + QKV_WIDTH]); o += QKV_WIDTH
    gate_ref[...] = _dot(h, w_ref[:, o:o + WIDTH]); o += WIDTH
    qb_ref[...] = (_dot(h, w_ref[:, o:o + WIDTH]) * (LOG2E * HEAD_DIM ** -0.5)).astype(BF16); o += WIDTH
    kb = _dot(h, w_ref[:, o:o + WIDTH]); o += WIDTH
    kb_ref[...] = kb
    kb16_ref[...] = kb.astype(BF16)
    vb = _dot(h, w_ref[:, o:o + WIDTH]); o += WIDTH
    vb_ref[...] = vb
    vb16_ref[...] = vb.astype(BF16)
    ab_ref[...] = _dot(h, w_ref[:, o:o + LANES])


def _inproj(x, lnw, w, tile):
    n, d = x.shape
    wcols = w.shape[1]
    row = lambda width: pl.BlockSpec((tile, width), lambda i: (i, 0))
    shp = lambda width, dt: jax.ShapeDtypeStruct((n, width), dt)
    return pl.pallas_call(
        _inproj_kernel,
        grid=(n // tile,),
        in_specs=[row(d), pl.BlockSpec((1, d), lambda i: (0, 0)),
                  pl.BlockSpec((d, wcols), lambda i: (0, 0))],
        out_specs=[row(QKV_WIDTH), row(WIDTH), row(LANES), row(WIDTH), row(WIDTH),
                   row(WIDTH), row(WIDTH), row(WIDTH)],
        out_shape=[shp(QKV_WIDTH, F32), shp(WIDTH, F32), shp(LANES, F32), shp(WIDTH, BF16),
                   shp(WIDTH, F32), shp(WIDTH, F32), shp(WIDTH, BF16), shp(WIDTH, BF16)],
        compiler_params=_params("parallel"),
        name="inproj",
    )(x, lnw, w)


def _silu(x):
    return x * (1.0 / (1.0 + jnp.exp(-x)))


def _softplus(x):
    return jnp.maximum(x, 0.0) + jnp.log1p(jnp.exp(-jnp.abs(x)))


def _dot_sel(x, sel):
    hi, mid, lo = _split3(x)
    return _dot(hi, sel) + _dot(mid, sel) + _dot(lo, sel)


def _unit_lower_inverses(lows, rows, block):
    ri = _iota((rows, rows), 0)
    ci = _iota((rows, rows), 1)
    invs = [(ri == ci).astype(F32)] * len(lows)
    s = 1
    while s < block:
        off = ((ri & s) != 0) & ((ci & s) == 0) & ((ri ^ ci) < 2 * s)
        blks = [jnp.where(off, low, 0.0).astype(BF16) for low in lows]
        inv16 = [inv.astype(BF16) for inv in invs]
        right = [_dot(blk, i16) for blk, i16 in zip(blks, inv16)]
        invs = [inv - _dot(i16, r.astype(BF16)) for inv, i16, r in zip(invs, inv16, right)]
        s *= 2
    return invs


def _gdn_kernel(qkv_ref, ab_ref, gate_ref, convw_ref, alog_ref, dtb_ref, normw_ref,
                s0_ref, conv0_ref, o_ref, sout_ref, convout_ref, xbuf, s_scr, *, chunk):
    c = pl.program_id(1)
    hist = CONV_TAPS - 1
    rows = QUAD * chunk
    chunk_shift = _log2_int(chunk)
    head_shift = _log2_int(HEAD_DIM)
    n_quads = N_HEADS // QUAD

    @pl.when(c == 0)
    def _():
        xbuf[0:8, :] = jnp.zeros((8, QKV_WIDTH), F32)
        xbuf[8 - hist:8, :] = conv0_ref[0]
        s_scr[...] = jnp.zeros_like(s_scr)
        for h in range(N_HEADS):
            blk = slice((h % QUAD) * HEAD_DIM, (h % QUAD + 1) * HEAD_DIM)
            s_scr[h // QUAD, blk, blk] = s0_ref[0, h]

    xbuf[8:8 + chunk, :] = qkv_ref[0]
    conv = convw_ref[0:1, :] * xbuf[8 - hist:8 - hist + chunk, :]
    for i in range(1, CONV_TAPS):
        conv = conv + convw_ref[i:i + 1, :] * xbuf[8 - hist + i:8 - hist + i + chunk, :]
    convout_ref[0] = xbuf[8 + chunk - hist:8 + chunk, :]
    tail = xbuf[chunk:chunk + 8, :]
    xbuf[0:8, :] = tail
    act = _silu(conv)

    ab = ab_ref[0]
    lane = _iota((chunk, LANES), 1)
    g = -jnp.exp(alog_ref[...]) * _softplus(ab + dtb_ref[...])
    beta = 1.0 / (1.0 + jnp.exp(-ab))
    tril = (_iota((chunk, chunk), 0) >= _iota((chunk, chunk), 1)).astype(BF16)
    g_hi, g_mid, g_lo = _split3(g)
    gcum = _dot(tril, g_hi) + _dot(tril, g_mid) + _dot(tril, g_lo)
    eg_beta = jnp.where(lane < N_HEADS, jnp.exp(gcum), beta)
    edec = jnp.exp(gcum[chunk - 1:chunk, :] - gcum)

    lane_head = lax.shift_right_logical(_iota((LANES, QUAD_WIDTH), 1), head_shift)
    src_lane = _iota((LANES, QUAD_WIDTH), 0)
    block_ones = (lax.shift_right_logical(_iota((QUAD_WIDTH, QUAD_WIDTH), 0), head_shift)
                  == lax.shift_right_logical(_iota((QUAD_WIDTH, QUAD_WIDTH), 1), head_shift)).astype(BF16)
    ones_k = jnp.ones((LANES, rows), BF16)
    ri = _iota((rows, rows), 0)
    ci = _iota((rows, rows), 1)
    same = lax.shift_right_logical(ri, chunk_shift) == lax.shift_right_logical(ci, chunk_shift)
    incl = same & (ri >= ci)
    strict = same & (ri > ci)
    exp_row_head = lax.shift_right_logical(_iota((rows, QUAD_WIDTH), 0), chunk_shift)
    exp_lane_head = lax.shift_right_logical(_iota((rows, QUAD_WIDTH), 1), head_shift)
    own_block = exp_row_head == exp_lane_head
    stack_row_head = lax.shift_right_logical(_iota((rows, LANES), 0), chunk_shift)
    stack_lane = _iota((rows, LANES), 1)

    def expand(x):
        return jnp.where(own_block, jnp.concatenate([x] * QUAD, axis=0), 0.0)

    quads = range(n_quads)
    ones_r = jnp.ones((rows, LANES), BF16)

    def prepare(quad):
        first = quad * QUAD
        sel_g = (src_lane == lane_head + first).astype(BF16)
        sel_b = (src_lane == lane_head + (N_HEADS + first)).astype(BF16)
        eg_w = _dot_sel(eg_beta, sel_g)
        beta_w = _dot_sel(eg_beta, sel_b)
        edec_w = _dot_sel(edec, sel_g)
        q = act[:, quad * QUAD_WIDTH:(quad + 1) * QUAD_WIDTH]
        k = act[:, WIDTH + quad * QUAD_WIDTH:WIDTH + (quad + 1) * QUAD_WIDTH]
        v = act[:, 2 * WIDTH + quad * QUAD_WIDTH:2 * WIDTH + (quad + 1) * QUAD_WIDTH]
        q = q * lax.rsqrt(_dot_sel(q * q, block_ones) + L2_EPS) * (HEAD_DIM ** -0.5)
        k = k * lax.rsqrt(_dot_sel(k * k, block_ones) + L2_EPS)
        kbeta = k * beta_w
        stacked = jnp.where(stack_lane == stack_row_head + first,
                            jnp.concatenate([gcum] * QUAD, axis=0), 0.0)
        s_hi, s_mid, s_lo = _split3(stacked)
        g_row = _dot(s_hi, ones_k) + _dot(s_mid, ones_k) + _dot(s_lo, ones_k)
        g_col = _dot_nt(ones_r, s_hi) + _dot_nt(ones_r, s_mid) + _dot_nt(ones_r, s_lo)
        decay = jnp.where(incl, jnp.exp(g_row - g_col), 0.0)
        e_k = expand(k).astype(BF16)
        low = jnp.where(strict, _dot_nt(expand(kbeta).astype(BF16), e_k) * decay, 0.0)
        return dict(decay=decay, e_k=e_k, low=low, eg_last=eg_w[chunk - 1:chunk, :],
                    e_q=expand(q).astype(BF16), e_qg=expand(q * eg_w).astype(BF16),
                    e_vb=expand(v * beta_w).astype(BF16), e_kg=expand(kbeta * eg_w).astype(BF16),
                    e_kd=expand(k * edec_w))

    pre = [prepare(quad) for quad in quads]
    invs = _unit_lower_inverses([p["low"] for p in pre], rows, chunk)
    invs = [inv.astype(BF16) for inv in invs]
    states = [s_scr[quad] for quad in quads]
    states16 = [st.astype(BF16) for st in states]
    sol_v = [_dot(invs[i], pre[i]["e_vb"]) for i in quads]
    sol_k = [_dot(invs[i], pre[i]["e_kg"]) for i in quads]
    attn = [_dot_nt(pre[i]["e_q"], pre[i]["e_k"]) * pre[i]["decay"] for i in quads]
    from_state = [_dot(pre[i]["e_qg"], states16[i]) for i in quads]
    u = [sol_v[i] - _mm(sol_k[i], states16[i]) for i in quads]
    e_o = [from_state[i] + _mm(attn[i], u[i]) for i in quads]
    for i in quads:
        s_scr[i] = states[i] * pre[i]["eg_last"] + lax.dot_general(
            pre[i]["e_kd"], u[i], (((0,), (0,)), ((), ())), preferred_element_type=F32)
    for i in quads:
        o = e_o[i][0:chunk, :]
        for h in range(1, QUAD):
            o = o + e_o[i][h * chunk:(h + 1) * chunk, :]
        on = o * lax.rsqrt(_dot_sel(o * o, block_ones) * (1.0 / HEAD_DIM) + RMS_EPS)
        cols = slice(i * QUAD_WIDTH, (i + 1) * QUAD_WIDTH)
        o_ref[0, :, cols] = (on * normw_ref[...] * _silu(gate_ref[0, :, cols])).astype(o_ref.dtype)

    @pl.when(c == pl.num_programs(1) - 1)
    def _():
        for h in range(N_HEADS):
            blk = slice((h % QUAD) * HEAD_DIM, (h % QUAD + 1) * HEAD_DIM)
            sout_ref[0, h] = s_scr[h // QUAD, blk, blk]


def _gdn(qkv, ab, gate, convw, alog, dtb, normw, s0, conv0, chunk):
    b, t, _ = qkv.shape
    n = t // chunk
    tok = lambda width: pl.BlockSpec((1, chunk, width), lambda i, c: (i, c, 0))
    const2 = lambda r, cdim: pl.BlockSpec((r, cdim), lambda i, c: (0, 0))
    hist = CONV_TAPS - 1
    state_spec = pl.BlockSpec((1, N_HEADS, HEAD_DIM, HEAD_DIM), lambda i, c: (i, 0, 0, 0))
    conv_spec = pl.BlockSpec((1, hist, QKV_WIDTH), lambda i, c: (i, 0, 0))
    return pl.pallas_call(
        functools.partial(_gdn_kernel, chunk=chunk),
        grid=(b, n),
        in_specs=[tok(QKV_WIDTH), tok(LANES), tok(WIDTH),
                  const2(CONV_TAPS, QKV_WIDTH), const2(1, LANES), const2(1, LANES),
                  const2(1, QUAD_WIDTH), state_spec, conv_spec],
        out_specs=[tok(WIDTH), state_spec, conv_spec],
        out_shape=[jax.ShapeDtypeStruct((b, t, WIDTH), BF16),
                   jax.ShapeDtypeStruct((b, N_HEADS, HEAD_DIM, HEAD_DIM), F32),
                   jax.ShapeDtypeStruct((b, hist, QKV_WIDTH), F32)],
        scratch_shapes=[pltpu.VMEM((chunk + 8, QKV_WIDTH), F32),
                        pltpu.VMEM((N_HEADS // QUAD, QUAD_WIDTH, QUAD_WIDTH), F32)],
        compiler_params=_params("parallel", "arbitrary"),
        name=f"gdn_c{chunk}",
    )(qkv, ab, gate, convw, alog, dtb, normw, s0, conv0)


def _suffix_total():
    r = _iota((SB_BLOCK, 2 * SB_BLOCK), 0)
    c = _iota((SB_BLOCK, 2 * SB_BLOCK), 1)
    return ((r > c) | (c >= SB_BLOCK)).astype(BF16)


def _sb_log_parts(z, mask):
    sp = jnp.log2(1.0 + jnp.exp2(-jnp.abs(z)))
    log_beta = jnp.minimum(z, 0.0) - sp
    log_stay = log_beta - z
    if mask is not None:
        log_stay = jnp.where(mask, log_stay, 0.0)
    return log_beta, log_stay.astype(BF16)


def _sb_logs(z, mask, suffix_total):
    log_beta, log_stay = _sb_log_parts(z, mask)
    return log_beta, _dot(log_stay, suffix_total)


def _sb_weights(logs, mask, run):
    log_beta, sums = logs
    w = jnp.exp2(log_beta + run + sums[:, :SB_BLOCK])
    if mask is not None:
        w = jnp.where(mask, w, 0.0)
    return w, run + sums[:, SB_BLOCK:]


def _sb_block(z, mask, run, suffix_total):
    return _sb_weights(_sb_logs(z, mask, suffix_total), mask, run)


def _sb_prompt_kernel(bias_ref, q_ref, k_ref, v_ref, o_ref, qm_ref, acc_ref, run_ref, *, pad):
    i = pl.program_id(1)
    pair_lanes = 2 * HEAD_DIM
    n_pairs = N_HEADS // 2
    ri = _iota((SB_BLOCK, SB_BLOCK), 0)
    ci = _iota((SB_BLOCK, SB_BLOCK), 1)
    suffix_total = _suffix_total()

    for p in range(n_pairs):
        qp = q_ref[0, :, p * pair_lanes:(p + 1) * pair_lanes]
        qm_ref[2 * p] = jnp.where(ci < HEAD_DIM, qp, jnp.zeros_like(qp))
        qm_ref[2 * p + 1] = jnp.where(ci >= HEAD_DIM, qp, jnp.zeros_like(qp))
    acc_ref[...] = jnp.zeros_like(acc_ref)
    run_ref[...] = jnp.zeros_like(run_ref)

    def visit(blocks):
        kv = []
        for j, _ in blocks:
            start = pl.multiple_of(j * SB_BLOCK, SB_BLOCK)
            kv.append([(k_ref[0, pl.ds(start, SB_BLOCK), p * pair_lanes:(p + 1) * pair_lanes],
                        v_ref[0, pl.ds(start, SB_BLOCK), p * pair_lanes:(p + 1) * pair_lanes])
                       for p in range(n_pairs)])
        heads = range(N_HEADS)
        nb = range(len(blocks))
        runs = [run_ref[h] for h in heads]
        zs = [[_dot_nt(qm_ref[h], kv[b][h // 2][0]) + bias_ref[h] * LOG2E for h in heads] for b in nb]
        logs = [[_sb_log_parts(zs[b][h], blocks[b][1]) for h in heads] for b in nb]
        sums = [[_dot(logs[b][h][1], suffix_total) for h in heads] for b in nb]
        ws = []
        for b in nb:
            ws.append([])
            for h in heads:
                w, runs[h] = _sb_weights((logs[b][h][0], sums[b][h]), blocks[b][1], runs[h])
                ws[b].append(w.astype(BF16))
        parts = [[_dot(ws[b][h], kv[b][h // 2][1]) for h in heads] for b in nb]
        for h in heads:
            run_ref[h] = runs[h]
            out = parts[0][h]
            for b in nb[1:]:
                out = out + parts[b][h]
            acc_ref[h] += out

    visit([(i, (ci < ri) & (ci + i * SB_BLOCK >= pad))])

    n_mid = jnp.maximum(i - 1, 0)
    odd = n_mid & 1

    @pl.when(odd == 1)
    def _():
        visit([(i - 1, None)])

    def middle(step, carry):
        j = i - 1 - odd - 2 * step
        visit([(j, None), (j - 1, None)])
        return carry

    lax.fori_loop(0, n_mid // 2, middle, 0)

    @pl.when(i > 0)
    def _():
        visit([(0, ci >= pad)])

    for p in range(n_pairs):
        o_ref[0, :, p * pair_lanes:(p + 1) * pair_lanes] = jnp.where(
            ci < HEAD_DIM, acc_ref[2 * p], acc_ref[2 * p + 1]).astype(o_ref.dtype)


def _sb_prompt(q, k, v, bias, pad):
    b, t, width = q.shape
    nq = t // SB_BLOCK
    resident = pl.BlockSpec((1, t, width), lambda bi, i: (bi, 0, 0), pipeline_mode=pl.Buffered(1))
    return pl.pallas_call(
        functools.partial(_sb_prompt_kernel, pad=pad),
        grid=(b, nq),
        in_specs=[pl.BlockSpec(memory_space=pltpu.SMEM),
                  pl.BlockSpec((1, SB_BLOCK, width), lambda bi, i: (bi, i, 0)),
                  resident, resident],
        out_specs=pl.BlockSpec((1, SB_BLOCK, width), lambda bi, i: (bi, i, 0)),
        out_shape=jax.ShapeDtypeStruct((b, t, width), BF16),
        scratch_shapes=[pltpu.VMEM((N_HEADS, SB_BLOCK, SB_BLOCK), BF16),
                        pltpu.VMEM((N_HEADS, SB_BLOCK, SB_BLOCK), F32),
                        pltpu.VMEM((N_HEADS, SB_BLOCK, SB_BLOCK), F32)],
        compiler_params=_params("parallel", "arbitrary"),
        name="sb_prompt",
    )(bias, q, k, v)


def _sb_sample_kernel(pt_ref, q_ref, bias_ref, knew_ref, vnew_ref, *refs, pages_per_step, t_new):
    k_refs = refs[:pages_per_step]
    v_refs = refs[pages_per_step:2 * pages_per_step]
    o_ref = refs[2 * pages_per_step]
    acc_ref, run_ref = refs[2 * pages_per_step + 1:]
    g = pl.program_id(1)
    rows = N_HEADS * t_new
    bias = bias_ref[...] * LOG2E
    suffix_total = _suffix_total()

    def visit(state, k_of_head, v_of_head, mask):
        run, accs = state
        z = jnp.concatenate([_dot_nt(q_ref[0, h], k_of_head(h)) for h in range(N_HEADS)], axis=0) + bias
        w, run = _sb_block(z, mask, run, suffix_total)
        return run, [accs[h] + _dot(w[h * t_new:(h + 1) * t_new, :].astype(BF16), v_of_head(h))
                     for h in range(N_HEADS)]

    def store_state(state):
        run_ref[...] = state[0]
        for h in range(N_HEADS):
            acc_ref[h] = state[1][h]

    @pl.when(g == 0)
    def _():
        ri = _iota((rows, LANES), 0)
        ci = _iota((rows, LANES), 1)
        zero = (jnp.zeros((rows, LANES), F32), [jnp.zeros((t_new, HEAD_DIM), F32)] * N_HEADS)
        store_state(visit(zero, lambda h: knew_ref[0, h], lambda h: vnew_ref[0, h],
                          ci < lax.rem(ri, t_new)))

    def head_rows(ref, h):
        flat = ref.reshape(PAGE * N_HEADS, HEAD_DIM)
        return flat[pl.ds(h, PAGE, stride=N_HEADS), :].astype(BF16)

    order = list(reversed(range(pages_per_step)))
    heads = range(N_HEADS)
    zs = [jnp.concatenate([_dot_nt(q_ref[0, h], head_rows(k_refs[p], h)) for h in heads], axis=0) + bias
          for p in order]
    logs = [_sb_log_parts(z, None) for z in zs]
    sums = [_dot(log_stay, suffix_total) for _, log_stay in logs]
    run = run_ref[...]
    ws = []
    for (log_beta, _), s in zip(logs, sums):
        w, run = _sb_weights((log_beta, s), None, run)
        ws.append(w)
    accs = [acc_ref[h] for h in heads]
    for w, p in zip(ws, order):
        accs = [accs[h] + _dot(w[h * t_new:(h + 1) * t_new, :].astype(BF16), head_rows(v_refs[p], h))
                for h in heads]
    store_state((run, accs))

    @pl.when(g == pl.num_programs(1) - 1)
    def _():
        for h in range(N_HEADS):
            o_ref[0, :, h * HEAD_DIM:(h + 1) * HEAD_DIM] = acc_ref[h].astype(o_ref.dtype)


def _sb_sample(page_table, q, bias_rows, knew, vnew, cache_k, cache_v, layer, pages_per_step):
    b, _, t_new, _ = q.shape
    rows = N_HEADS * t_new
    n_pages = page_table.shape[1]
    n_groups = n_pages // pages_per_step

    def page_spec(p):
        def index(bi, g, pt):
            return (layer, pt[bi, (n_groups - 1 - g) * pages_per_step + p], 0, 0, 0)
        return pl.BlockSpec((1, 1, PAGE, N_HEADS, HEAD_DIM), index)

    per_batch = lambda r: pl.BlockSpec((1, N_HEADS, r, HEAD_DIM), lambda bi, g, pt: (bi, 0, 0, 0))
    return pl.pallas_call(
        functools.partial(_sb_sample_kernel, pages_per_step=pages_per_step, t_new=t_new),
        grid_spec=pltpu.PrefetchScalarGridSpec(
            num_scalar_prefetch=1,
            grid=(b, n_groups),
            in_specs=[per_batch(t_new),
                      pl.BlockSpec((rows, LANES), lambda bi, g, pt: (0, 0)),
                      per_batch(PAGE), per_batch(PAGE)]
                     + [page_spec(p) for p in range(pages_per_step)] * 2,
            out_specs=pl.BlockSpec((1, t_new, WIDTH), lambda bi, g, pt: (bi, 0, 0)),
            scratch_shapes=[pltpu.VMEM((N_HEADS, t_new, HEAD_DIM), F32),
                            pltpu.VMEM((rows, LANES), F32)],
        ),
        out_shape=jax.ShapeDtypeStruct((b, t_new, WIDTH), BF16),
        compiler_params=_params("parallel", "arbitrary"),
        name="sb_sample",
    )(page_table, q, bias_rows, knew, vnew,
      *([cache_k] * pages_per_step), *([cache_v] * pages_per_step))


def _outproj_kernel(x_ref, oa_ref, ob_ref, w_ref, y_ref):
    y_ref[...] = (x_ref[...] + _dot(oa_ref[...], w_ref[0:WIDTH, :])
                  + _dot(ob_ref[...], w_ref[WIDTH:2 * WIDTH, :]))


def _outproj(x, oa, ob, w, tile):
    n, d = x.shape
    return pl.pallas_call(
        _outproj_kernel,
        grid=(n // tile,),
        in_specs=[pl.BlockSpec((tile, d), lambda i: (i, 0)),
                  pl.BlockSpec((tile, WIDTH), lambda i: (i, 0)),
                  pl.BlockSpec((tile, WIDTH), lambda i: (i, 0)),
                  pl.BlockSpec((2 * WIDTH, d), lambda i: (0, 0))],
        out_specs=pl.BlockSpec((tile, d), lambda i: (i, 0)),
        out_shape=jax.ShapeDtypeStruct((n, d), F32),
        compiler_params=_params("parallel"),
        name="outproj",
    )(x, oa, ob, w)


def _top_values(x, count):
    vals = []
    for _ in range(count):
        m = jnp.max(x, axis=0, keepdims=True)
        vals.append(m)
        x = jnp.where(x == m, -jnp.inf, x)
    return jnp.concatenate(vals, axis=0)


def _route_kernel(x_ref, lnw_ref, wq_hi_ref, wq_mid_ref, wq_lo_ref, k1_ref, k2_ref,
                  h_ref, e1_ref, thr_ref, e2_ref):
    x = x_ref[...]
    hn = x * lax.rsqrt(jnp.mean(x * x, axis=-1, keepdims=True) + RMS_EPS) * lnw_ref[...]
    h_hi, h_mid, h_lo = _split3(hn)
    h_ref[...] = h_hi
    half = PEER_KEYS

    def query(cols):
        w_hi, w_mid, w_lo = wq_hi_ref[:, cols], wq_mid_ref[:, cols], wq_lo_ref[:, cols]
        return (_dot(h_lo, w_hi) + _dot(h_hi, w_lo) + _dot(h_mid, w_mid)
                + _dot(h_mid, w_hi) + _dot(h_hi, w_mid) + _dot(h_hi, w_hi))

    few = PEER_TOPK // 2

    head_cols = lambda h: slice(2 * h * half, (2 * h + 2) * half)
    q_next = query(head_cols(0))
    for h in range(N_HEADS):
        q = q_next
        if h + 1 < N_HEADS:
            q_next = query(head_cols(h + 1))
        q1, q2 = q[:, :half], q[:, half:]
        s1 = _dot_nt(k1_ref[...], q1, HIGHEST)
        s2 = _dot_nt(k2_ref[...], q2, HIGHEST)
        v1 = _top_values(s1, PEER_TOPK)
        v2 = _top_values(s2, PEER_TOPK)
        top = v1[0:1, :] + v2[0:1, :]
        e2_top = jnp.exp(v2 - v2[0:1, :])
        rows = [(v1[0:1, :] + v2, e2_top)]
        rows += [(v1[i:i + 1, :] + v2[0:few, :], e2_top[0:few, :]) for i in range(1, few)]
        tail = v1[few:, :] + v2[0:1, :]
        kth = _top_values(jnp.concatenate([r[0] for r in rows] + [tail], axis=0),
                          PEER_TOPK)[PEER_TOPK - 1:PEER_TOPK, :]
        z = jnp.zeros_like(top)
        thr = jnp.full_like(s1, jnp.inf)
        for i, (sums, e2_row) in enumerate(rows):
            picked = sums >= kth
            z = z + jnp.sum(jnp.where(picked, jnp.exp(sums - top), 0.0), axis=0, keepdims=True)
            row_thr = jnp.min(jnp.where(picked, e2_row, jnp.inf), axis=0, keepdims=True)
            thr = jnp.where(s1 == v1[i:i + 1, :], row_thr, thr)
        picked = tail >= kth
        z = z + jnp.sum(jnp.where(picked, jnp.exp(tail - top), 0.0), axis=0, keepdims=True)
        tail_thr = jnp.where(picked, e2_top[0:1, :], jnp.inf)
        for r in range(PEER_TOPK - few):
            thr = jnp.where(s1 == v1[few + r:few + r + 1, :], tail_thr[r:r + 1, :], thr)
        e1_ref[h] = jnp.exp(s1 - v1[0:1, :]) / z
        thr_ref[h] = thr
        e2_ref[h] = jnp.exp(s2 - v2[0:1, :])


def _route(x, lnw, wq, k1, k2, tile):
    wq_pieces = _split3(wq)
    wq_spec = pl.BlockSpec(wq.shape, lambda i: (0, 0), pipeline_mode=pl.Buffered(1))
    n, d = x.shape
    per_head = pl.BlockSpec((N_HEADS, PEER_KEYS, tile), lambda i: (0, 0, i))
    head_shape = jax.ShapeDtypeStruct((N_HEADS, PEER_KEYS, n), F32)
    return pl.pallas_call(
        _route_kernel,
        grid=(n // tile,),
        in_specs=[pl.BlockSpec((tile, d), lambda i: (i, 0)),
                  pl.BlockSpec((1, d), lambda i: (0, 0)),
                  wq_spec, wq_spec, wq_spec,
                  pl.BlockSpec(k1.shape, lambda i: (0, 0)),
                  pl.BlockSpec(k2.shape, lambda i: (0, 0))],
        out_specs=[pl.BlockSpec((tile, d), lambda i: (i, 0)), per_head, per_head, per_head],
        out_shape=[jax.ShapeDtypeStruct((n, d), BF16), head_shape, head_shape, head_shape],
        compiler_params=_params("parallel"),
        name="peer_route",
    )(x, lnw, *wq_pieces, k1, k2)


def _gelu_tanh(x):
    c = 0.7978845608028654
    return 0.5 * x * (1.0 + jnp.tanh(c * (x + 0.044715 * (x * x * x))))


def _peer_kernel(x_ref, h_ref, u_ref, vt_ref, e1_ref, thr_ref, e2_ref, y_ref, acc_ref, w_ref,
                 *, expert_tile):
    j = pl.program_id(1)
    tile = h_ref.shape[0]
    groups = expert_tile // PEER_KEYS

    @pl.when(j == 0)
    def _():
        acc_ref[...] = jnp.zeros_like(acc_ref)

    act = _gelu_tanh(_dot_nt(u_ref[...], h_ref[...]))
    for a_local in range(groups):
        a = j * groups + a_local
        gate = jnp.zeros((PEER_KEYS, tile), F32)
        for h in range(N_HEADS):
            e2 = e2_ref[h]
            gate = gate + jnp.where(e2 >= thr_ref[h, pl.ds(a, 1), :], e2, 0.0) * e1_ref[h, pl.ds(a, 1), :]
        rows = slice(a_local * PEER_KEYS, (a_local + 1) * PEER_KEYS)
        w_ref[rows, :] = (gate * act[rows, :]).astype(BF16)
    acc_ref[...] += _dot(vt_ref[...], w_ref[...])

    @pl.when(j == pl.num_programs(1) - 1)
    def _():
        y_ref[...] = x_ref[...] + acc_ref[...].T


def _peer(x, hn, u, vt, e1, thr, e2, tile, expert_tile):
    n, d = x.shape
    per_head = pl.BlockSpec((N_HEADS, PEER_KEYS, tile), lambda i, j: (0, 0, i))
    return pl.pallas_call(
        functools.partial(_peer_kernel, expert_tile=expert_tile),
        grid=(n // tile, u.shape[0] // expert_tile),
        in_specs=[pl.BlockSpec((tile, d), lambda i, j: (i, 0)),
                  pl.BlockSpec((tile, d), lambda i, j: (i, 0)),
                  pl.BlockSpec((expert_tile, d), lambda i, j: (j, 0)),
                  pl.BlockSpec((d, expert_tile), lambda i, j: (0, j)),
                  per_head, per_head, per_head],
        out_specs=pl.BlockSpec((tile, d), lambda i, j: (i, 0)),
        out_shape=jax.ShapeDtypeStruct((n, d), F32),
        scratch_shapes=[pltpu.VMEM((d, tile), F32), pltpu.VMEM((expert_tile, tile), BF16)],
        compiler_params=_params("parallel", "arbitrary"),
        name="peer_dense",
    )(x, hn, u, vt, e1, thr, e2)


def _norm_kernel(x_ref, w_ref, y_ref):
    x = x_ref[...]
    y_ref[...] = x * lax.rsqrt(jnp.mean(x * x, axis=-1, keepdims=True) + RMS_EPS) * w_ref[...]


def _final_norm(x, w, tile):
    n, d = x.shape
    return pl.pallas_call(
        _norm_kernel,
        grid=(n // tile,),
        in_specs=[pl.BlockSpec((tile, d), lambda i: (i, 0)), pl.BlockSpec((1, d), lambda i: (0, 0))],
        out_specs=pl.BlockSpec((tile, d), lambda i: (i, 0)),
        out_shape=jax.ShapeDtypeStruct((n, d), F32),
        compiler_params=_params("parallel"),
        name="final_norm",
    )(x, w)


ROW_TILE = 256
PEER_EXPERT_TILE = 1024
PAGES_PER_STEP = 8


def _heads_major(a):
    b, t, _ = a.shape
    return a.reshape(b, t, N_HEADS, HEAD_DIM).transpose(0, 2, 1, 3)


def _pad_lanes(a, width):
    return jnp.pad(a, (0, width - a.shape[0]))[None]


def kernel(x_prompt, x_sample, cache_sb_k, cache_sb_v, page_table, state_gdn, state_conv,
           meta_tokens, ln1_w, ln2_w, lnf_w, w_in, conv_w, gdn_a_log, gdn_dt_bias, gdn_norm_w,
           sb_logit_bias, w_out, peer_wq, peer_k1, peer_k2, peer_u, peer_v):
    bp, seq, d = x_prompt.shape
    bs, ts, _ = x_sample.shape
    depth = w_in.shape[0]
    pad = (-N_META) % SB_BLOCK
    tp = pad + N_META + seq
    np_rows = bp * tp
    ns_rows = bs * ts
    hist = CONV_TAPS - 1

    meta = jnp.broadcast_to(meta_tokens[None], (bp, N_META, d))
    xp = jnp.concatenate([jnp.zeros((bp, pad, d), F32), meta, x_prompt], axis=1)
    x = jnp.concatenate([xp.reshape(np_rows, d), x_sample.reshape(ns_rows, d)], axis=0)

    zero_state = jnp.zeros((bp, N_HEADS, HEAD_DIM, HEAD_DIM), F32)
    zero_conv = jnp.zeros((bp, hist, QKV_WIDTH), F32)

    outs = {k: [] for k in ("kp", "vp", "ks", "vs", "sp", "ss", "cp", "cs")}
    for l in range(depth):
        wl = w_in[l]
        o_gate = QKV_WIDTH
        o_a = o_gate + WIDTH
        o_qb = o_a + 2 * N_HEADS
        w_perm = jnp.concatenate(
            [wl[:, :o_a], wl[:, o_qb:], wl[:, o_a:o_qb],
             jnp.zeros((d, LANES - 2 * N_HEADS), F32)], axis=1).astype(BF16)
        qkv, gate, ab, qb, kb, vb, kb16, vb16 = _inproj(x, ln1_w[l][None], w_perm, ROW_TILE)

        gdn_consts = (conv_w[l], _pad_lanes(gdn_a_log[l], LANES), _pad_lanes(gdn_dt_bias[l], LANES),
                      jnp.tile(gdn_norm_w[l], QUAD)[None])
        split = lambda a, w_: (a[:np_rows].reshape(bp, tp, w_), a[np_rows:].reshape(bs, ts, w_))
        qkv_p, qkv_s = split(qkv, QKV_WIDTH)
        ab_p, ab_s = split(ab, LANES)
        gate_p, gate_s = split(gate, WIDTH)
        oa_p, s_p, c_p = _gdn(qkv_p, ab_p, gate_p, *gdn_consts, zero_state, zero_conv, GDN_CHUNK)
        oa_s, s_s, c_s = _gdn(qkv_s, ab_s, gate_s, *gdn_consts, state_gdn[l], state_conv[l], ts)

        qb_p, qb_s = split(qb, WIDTH)
        kb16_p, kb16_s = split(kb16, WIDTH)
        vb16_p, vb16_s = split(vb16, WIDTH)
        ob_p = _sb_prompt(qb_p, kb16_p, vb16_p, sb_logit_bias[l], pad).reshape(np_rows, WIDTH)

        bias_rows = jnp.broadcast_to(jnp.repeat(sb_logit_bias[l], ts)[:, None], (N_HEADS * ts, LANES))
        grow = lambda a: jnp.pad(_heads_major(a), ((0, 0), (0, 0), (0, PAGE - ts), (0, 0)))
        ob_s = _sb_sample(page_table, _heads_major(qb_s), bias_rows, grow(kb16_s), grow(vb16_s),
                          cache_sb_k, cache_sb_v, l, PAGES_PER_STEP).reshape(ns_rows, WIDTH)

        oa = jnp.concatenate([oa_p.reshape(np_rows, WIDTH), oa_s.reshape(ns_rows, WIDTH)], axis=0)
        ob = jnp.concatenate([ob_p, ob_s], axis=0)
        x = _outproj(x, oa, ob, w_out[l].astype(BF16), ROW_TILE)

        hn, e1, thr, e2 = _route(x, ln2_w[l][None], peer_wq[l], peer_k1[l], peer_k2[l], ROW_TILE)
        x = _peer(x, hn, peer_u[l].astype(BF16), peer_v[l].astype(BF16).T, e1, thr, e2,
                  ROW_TILE, PEER_EXPERT_TILE)

        kb_p, kb_s = split(kb, WIDTH)
        vb_p, vb_s = split(vb, WIDTH)
        outs["kp"].append(kb_p[:, pad:].reshape(bp, tp - pad, N_HEADS, HEAD_DIM))
        outs["vp"].append(vb_p[:, pad:].reshape(bp, tp - pad, N_HEADS, HEAD_DIM))
        outs["ks"].append(kb_s.reshape(bs, ts, N_HEADS, HEAD_DIM))
        outs["vs"].append(vb_s.reshape(bs, ts, N_HEADS, HEAD_DIM))
        outs["sp"].append(s_p); outs["ss"].append(s_s)
        outs["cp"].append(c_p); outs["cs"].append(c_s)

    y = _final_norm(x, lnf_w[None], ROW_TILE)
    y_prompt = y[:np_rows].reshape(bp, tp, d)[:, pad + N_META:]
    y_sample = y[np_rows:].reshape(bs, ts, d)
    stack = lambda key: jnp.stack(outs[key])
    return (y_prompt, y_sample, stack("kp"), stack("vp"), stack("ks"), stack("vs"),
            stack("sp"), stack("ss"), stack("cp"), stack("cs"))
```

```python
import functools

import jax
import jax.numpy as jnp
from jax import lax
from jax.experimental import pallas as pl
from jax.experimental.pallas import tpu as pltpu

F32 = jnp.float32
BF16 = jnp.bfloat16
HIGHEST = lax.Precision.HIGHEST

HEAD_DIM = 64
N_HEADS = 8
WIDTH = N_HEADS * HEAD_DIM
QKV_WIDTH = 3 * WIDTH
QUAD = 4
QUAD_WIDTH = QUAD * HEAD_DIM
CONV_TAPS = 4
N_META = 16
GDN_CHUNK = 64
SB_BLOCK = 128
PAGE = 128
PEER_KEYS = 128
PEER_TOPK = 16
RMS_EPS = 1e-6
L2_EPS = 1e-6
LOG2E = 1.4426950408889634
LANES = 128
VMEM_LIMIT = 56 * 1024 * 1024


def _dot(a, b, precision=None):
    return jnp.dot(a, b, preferred_element_type=F32, precision=precision)


def _dot_nt(a, b, precision=None):
    return lax.dot_general(a, b, (((1,), (1,)), ((), ())),
                           preferred_element_type=F32, precision=precision)


def _mm(a, b):
    return _dot(a.astype(BF16), b.astype(BF16))


def _mm_nt(a, b):
    return _dot_nt(a.astype(BF16), b.astype(BF16))


def _split3(x):
    hi = x.astype(BF16)
    rest = x - hi.astype(F32)
    mid = rest.astype(BF16)
    lo = (rest - mid.astype(F32)).astype(BF16)
    return hi, mid, lo


def _params(*sem):
    return pltpu.CompilerParams(dimension_semantics=sem, vmem_limit_bytes=VMEM_LIMIT)


def _iota(shape, axis):
    return lax.broadcasted_iota(jnp.int32, shape, axis)


def _log2_int(n):
    assert n & (n - 1) == 0, n
    return n.bit_length() - 1


def _inproj_kernel(x_ref, lnw_ref, w_ref, qkv_ref, gate_ref, ab_ref, qb_ref,
                   kb_ref, vb_ref, kb16_ref, vb16_ref):
    x = x_ref[...]
    h = x * lax.rsqrt(jnp.mean(x * x, axis=-1, keepdims=True) + RMS_EPS) * lnw_ref[...]
    h = h.astype(BF16)
    o = 0
    qkv_ref[...] = _dot(h, w_ref[:, o:o + QKV_WIDTH]); o += QKV_WIDTH
    gate_ref[...] = _dot(h, w_ref[:, o:o + WIDTH]); o += WIDTH
    qb_ref[...] = (_dot(h, w_ref[:, o:o + WIDTH]) * (LOG2E * HEAD_DIM ** -0.5)).astype(BF16); o += WIDTH
    kb = _dot(h, w_ref[:, o:o + WIDTH]); o += WIDTH
    kb_ref[...] = kb
    kb16_ref[...] = kb.astype(BF16)
    vb = _dot(h, w_ref[:, o:o + WIDTH]); o += WIDTH
    vb_ref[...] = vb
    vb16_ref[...] = vb.astype(BF16)
    ab_ref[...] = _dot(h, w_ref[:, o:o + LANES])


def _inproj(x, lnw, w, tile):
    n, d = x.shape
    wcols = w.shape[1]
    row = lambda width: pl.BlockSpec((tile, width), lambda i: (i, 0))
    shp = lambda width, dt: jax.ShapeDtypeStruct((n, width), dt)
    return pl.pallas_call(
        _inproj_kernel,
        grid=(n // tile,),
        in_specs=[row(d), pl.BlockSpec((1, d), lambda i: (0, 0)),
                  pl.BlockSpec((d, wcols), lambda i: (0, 0))],
        out_specs=[row(QKV_WIDTH), row(WIDTH), row(LANES), row(WIDTH), row(WIDTH),
                   row(WIDTH), row(WIDTH), row(WIDTH)],
        out_shape=[shp(QKV_WIDTH, F32), shp(WIDTH, F32), shp(LANES, F32), shp(WIDTH, BF16),
                   shp(WIDTH, F32), shp(WIDTH, F32), shp(WIDTH, BF16), shp(WIDTH, BF16)],
        compiler_params=_params("parallel"),
        name="inproj",
    )(x, lnw, w)


def _silu(x):
    return x * (1.0 / (1.0 + jnp.exp(-x)))


def _softplus(x):
    return jnp.maximum(x, 0.0) + jnp.log1p(jnp.exp(-jnp.abs(x)))


def _dot_sel(x, sel):
    hi, mid, lo = _split3(x)
    return _dot(hi, sel) + _dot(mid, sel) + _dot(lo, sel)


def _unit_lower_inverses(lows, rows, block):
    ri = _iota((rows, rows), 0)
    ci = _iota((rows, rows), 1)
    invs = [(ri == ci).astype(F32)] * len(lows)
    s = 1
    while s < block:
        off = ((ri & s) != 0) & ((ci & s) == 0) & ((ri ^ ci) < 2 * s)
        blks = [jnp.where(off, low, 0.0).astype(BF16) for low in lows]
        inv16 = [inv.astype(BF16) for inv in invs]
        right = [_dot(blk, i16) for blk, i16 in zip(blks, inv16)]
        invs = [inv - _dot(i16, r.astype(BF16)) for inv, i16, r in zip(invs, inv16, right)]
        s *= 2
    return invs


def _gdn_kernel(qkv_ref, ab_ref, gate_ref, convw_ref, alog_ref, dtb_ref, normw_ref,
                s0_ref, conv0_ref, o_ref, sout_ref, convout_ref, xbuf, s_scr, *, chunk):
    c = pl.program_id(1)
    hist = CONV_TAPS - 1
    rows = QUAD * chunk
    chunk_shift = _log2_int(chunk)
    head_shift = _log2_int(HEAD_DIM)
    n_quads = N_HEADS // QUAD

    @pl.when(c == 0)
    def _():
        xbuf[0:8, :] = jnp.zeros((8, QKV_WIDTH), F32)
        xbuf[8 - hist:8, :] = conv0_ref[0]
        s_scr[...] = jnp.zeros_like(s_scr)
        for h in range(N_HEADS):
            blk = slice((h % QUAD) * HEAD_DIM, (h % QUAD + 1) * HEAD_DIM)
            s_scr[h // QUAD, blk, blk] = s0_ref[0, h]

    xbuf[8:8 + chunk, :] = qkv_ref[0]
    conv = convw_ref[0:1, :] * xbuf[8 - hist:8 - hist + chunk, :]
    for i in range(1, CONV_TAPS):
        conv = conv + convw_ref[i:i + 1, :] * xbuf[8 - hist + i:8 - hist + i + chunk, :]
    convout_ref[0] = xbuf[8 + chunk - hist:8 + chunk, :]
    tail = xbuf[chunk:chunk + 8, :]
    xbuf[0:8, :] = tail
    act = _silu(conv)

    ab = ab_ref[0]
    lane = _iota((chunk, LANES), 1)
    g = -jnp.exp(alog_ref[...]) * _softplus(ab + dtb_ref[...])
    beta = 1.0 / (1.0 + jnp.exp(-ab))
    tril = (_iota((chunk, chunk), 0) >= _iota((chunk, chunk), 1)).astype(BF16)
    g_hi, g_mid, g_lo = _split3(g)
    gcum = _dot(tril, g_hi) + _dot(tril, g_mid) + _dot(tril, g_lo)
    eg_beta = jnp.where(lane < N_HEADS, jnp.exp(gcum), beta)
    edec = jnp.exp(gcum[chunk - 1:chunk, :] - gcum)

    lane_head = lax.shift_right_logical(_iota((LANES, QUAD_WIDTH), 1), head_shift)
    src_lane = _iota((LANES, QUAD_WIDTH), 0)
    block_ones = (lax.shift_right_logical(_iota((QUAD_WIDTH, QUAD_WIDTH), 0), head_shift)
                  == lax.shift_right_logical(_iota((QUAD_WIDTH, QUAD_WIDTH), 1), head_shift)).astype(BF16)
    ones_k = jnp.ones((LANES, rows), BF16)
    ri = _iota((rows, rows), 0)
    ci = _iota((rows, rows), 1)
    same = lax.shift_right_logical(ri, chunk_shift) == lax.shift_right_logical(ci, chunk_shift)
    incl = same & (ri >= ci)
    strict = same & (ri > ci)
    exp_row_head = lax.shift_right_logical(_iota((rows, QUAD_WIDTH), 0), chunk_shift)
    exp_lane_head = lax.shift_right_logical(_iota((rows, QUAD_WIDTH), 1), head_shift)
    own_block = exp_row_head == exp_lane_head
    stack_row_head = lax.shift_right_logical(_iota((rows, LANES), 0), chunk_shift)
    stack_lane = _iota((rows, LANES), 1)

    def expand(x):
        return jnp.where(own_block, jnp.concatenate([x] * QUAD, axis=0), 0.0)

    quads = range(n_quads)
    ones_r = jnp.ones((rows, LANES), BF16)

    def prepare(quad):
        first = quad * QUAD
        sel_g = (src_lane == lane_head + first).astype(BF16)
        sel_b = (src_lane == lane_head + (N_HEADS + first)).astype(BF16)
        eg_w = _dot_sel(eg_beta, sel_g)
        beta_w = _dot_sel(eg_beta, sel_b)
        edec_w = _dot_sel(edec, sel_g)
        q = act[:, quad * QUAD_WIDTH:(quad + 1) * QUAD_WIDTH]
        k = act[:, WIDTH + quad * QUAD_WIDTH:WIDTH + (quad + 1) * QUAD_WIDTH]
        v = act[:, 2 * WIDTH + quad * QUAD_WIDTH:2 * WIDTH + (quad + 1) * QUAD_WIDTH]
        q = q * lax.rsqrt(_dot_sel(q * q, block_ones) + L2_EPS) * (HEAD_DIM ** -0.5)
        k = k * lax.rsqrt(_dot_sel(k * k, block_ones) + L2_EPS)
        kbeta = k * beta_w
        stacked = jnp.where(stack_lane == stack_row_head + first,
                            jnp.concatenate([gcum] * QUAD, axis=0), 0.0)
        s_hi, s_mid, s_lo = _split3(stacked)
        g_row = _dot(s_hi, ones_k) + _dot(s_mid, ones_k) + _dot(s_lo, ones_k)
        g_col = _dot_nt(ones_r, s_hi) + _dot_nt(ones_r, s_mid) + _dot_nt(ones_r, s_lo)
        decay = jnp.where(incl, jnp.exp(g_row - g_col), 0.0)
        e_k = expand(k).astype(BF16)
        low = jnp.where(strict, _dot_nt(expand(kbeta).astype(BF16), e_k) * decay, 0.0)
        return dict(decay=decay, e_k=e_k, low=low, eg_last=eg_w[chunk - 1:chunk, :],
                    e_q=expand(q).astype(BF16), e_qg=expand(q * eg_w).astype(BF16),
                    e_vb=expand(v * beta_w).astype(BF16), e_kg=expand(kbeta * eg_w).astype(BF16),
                    e_kd=expand(k * edec_w))

    pre = [prepare(quad) for quad in quads]
    invs = _unit_lower_inverses([p["low"] for p in pre], rows, chunk)
    invs = [inv.astype(BF16) for inv in invs]
    states = [s_scr[quad] for quad in quads]
    states16 = [st.astype(BF16) for st in states]
    sol_v = [_dot(invs[i], pre[i]["e_vb"]) for i in quads]
    sol_k = [_dot(invs[i], pre[i]["e_kg"]) for i in quads]
    attn = [_dot_nt(pre[i]["e_q"], pre[i]["e_k"]) * pre[i]["decay"] for i in quads]
    from_state = [_dot(pre[i]["e_qg"], states16[i]) for i in quads]
    u = [sol_v[i] - _mm(sol_k[i], states16[i]) for i in quads]
    e_o = [from_state[i] + _mm(attn[i], u[i]) for i in quads]
    for i in quads:
        s_scr[i] = states[i] * pre[i]["eg_last"] + lax.dot_general(
            pre[i]["e_kd"], u[i], (((0,), (0,)), ((), ())), preferred_element_type=F32)
    for i in quads:
        o = e_o[i][0:chunk, :]
        for h in range(1, QUAD):
            o = o + e_o[i][h * chunk:(h + 1) * chunk, :]
        on = o * lax.rsqrt(_dot_sel(o * o, block_ones) * (1.0 / HEAD_DIM) + RMS_EPS)
        cols = slice(i * QUAD_WIDTH, (i + 1) * QUAD_WIDTH)
        o_ref[0, :, cols] = (on * normw_ref[...] * _silu(gate_ref[0, :, cols])).astype(o_ref.dtype)

    @pl.when(c == pl.num_programs(1) - 1)
    def _():
        for h in range(N_HEADS):
            blk = slice((h % QUAD) * HEAD_DIM, (h % QUAD + 1) * HEAD_DIM)
            sout_ref[0, h] = s_scr[h // QUAD, blk, blk]


def _gdn(qkv, ab, gate, convw, alog, dtb, normw, s0, conv0, chunk):
    b, t, _ = qkv.shape
    n = t // chunk
    tok = lambda width: pl.BlockSpec((1, chunk, width), lambda i, c: (i, c, 0))
    const2 = lambda r, cdim: pl.BlockSpec((r, cdim), lambda i, c: (0, 0))
    hist = CONV_TAPS - 1
    state_spec = pl.BlockSpec((1, N_HEADS, HEAD_DIM, HEAD_DIM), lambda i, c: (i, 0, 0, 0))
    conv_spec = pl.BlockSpec((1, hist, QKV_WIDTH), lambda i, c: (i, 0, 0))
    return pl.pallas_call(
        functools.partial(_gdn_kernel, chunk=chunk),
        grid=(b, n),
        in_specs=[tok(QKV_WIDTH), tok(LANES), tok(WIDTH),
                  const2(CONV_TAPS, QKV_WIDTH), const2(1, LANES), const2(1, LANES),
                  const2(1, QUAD_WIDTH), state_spec, conv_spec],
        out_specs=[tok(WIDTH), state_spec, conv_spec],
        out_shape=[jax.ShapeDtypeStruct((b, t, WIDTH), BF16),
                   jax.ShapeDtypeStruct((b, N_HEADS, HEAD_DIM, HEAD_DIM), F32),
                   jax.ShapeDtypeStruct((b, hist, QKV_WIDTH), F32)],
        scratch_shapes=[pltpu.VMEM((chunk + 8, QKV_WIDTH), F32),
                        pltpu.VMEM((N_HEADS // QUAD, QUAD_WIDTH, QUAD_WIDTH), F32)],
        compiler_params=_params("parallel", "arbitrary"),
        name=f"gdn_c{chunk}",
    )(qkv, ab, gate, convw, alog, dtb, normw, s0, conv0)


def _suffix_total():
    r = _iota((SB_BLOCK, 2 * SB_BLOCK), 0)
    c = _iota((SB_BLOCK, 2 * SB_BLOCK), 1)
    return ((r > c) | (c >= SB_BLOCK)).astype(BF16)


def _sb_log_parts(z, mask):
    sp = jnp.log2(1.0 + jnp.exp2(-jnp.abs(z)))
    log_beta = jnp.minimum(z, 0.0) - sp
    log_stay = log_beta - z
    if mask is not None:
        log_stay = jnp.where(mask, log_stay, 0.0)
    return log_beta, log_stay.astype(BF16)


def _sb_logs(z, mask, suffix_total):
    log_beta, log_stay = _sb_log_parts(z, mask)
    return log_beta, _dot(log_stay, suffix_total)


def _sb_weights(logs, mask, run):
    log_beta, sums = logs
    w = jnp.exp2(log_beta + run + sums[:, :SB_BLOCK])
    if mask is not None:
        w = jnp.where(mask, w, 0.0)
    return w, run + sums[:, SB_BLOCK:]


def _sb_block(z, mask, run, suffix_total):
    return _sb_weights(_sb_logs(z, mask, suffix_total), mask, run)


def _sb_prompt_kernel(bias_ref, q_ref, k_ref, v_ref, o_ref, qm_ref, acc_ref, run_ref, *, pad):
    i = pl.program_id(1)
    pair_lanes = 2 * HEAD_DIM
    n_pairs = N_HEADS // 2
    ri = _iota((SB_BLOCK, SB_BLOCK), 0)
    ci = _iota((SB_BLOCK, SB_BLOCK), 1)
    suffix_total = _suffix_total()

    for p in range(n_pairs):
        qp = q_ref[0, :, p * pair_lanes:(p + 1) * pair_lanes]
        qm_ref[2 * p] = jnp.where(ci < HEAD_DIM, qp, jnp.zeros_like(qp))
        qm_ref[2 * p + 1] = jnp.where(ci >= HEAD_DIM, qp, jnp.zeros_like(qp))
    acc_ref[...] = jnp.zeros_like(acc_ref)
    run_ref[...] = jnp.zeros_like(run_ref)

    def visit(blocks):
        kv = []
        for j, _ in blocks:
            start = pl.multiple_of(j * SB_BLOCK, SB_BLOCK)
            kv.append([(k_ref[0, pl.ds(start, SB_BLOCK), p * pair_lanes:(p + 1) * pair_lanes],
                        v_ref[0, pl.ds(start, SB_BLOCK), p * pair_lanes:(p + 1) * pair_lanes])
                       for p in range(n_pairs)])
        heads = range(N_HEADS)
        nb = range(len(blocks))
        runs = [run_ref[h] for h in heads]
        zs = [[_dot_nt(qm_ref[h], kv[b][h // 2][0]) + bias_ref[h] * LOG2E for h in heads] for b in nb]
        logs = [[_sb_log_parts(zs[b][h], blocks[b][1]) for h in heads] for b in nb]
        sums = [[_dot(logs[b][h][1], suffix_total) for h in heads] for b in nb]
        ws = []
        for b in nb:
            ws.append([])
            for h in heads:
                w, runs[h] = _sb_weights((logs[b][h][0], sums[b][h]), blocks[b][1], runs[h])
                ws[b].append(w.astype(BF16))
        parts = [[_dot(ws[b][h], kv[b][h // 2][1]) for h in heads] for b in nb]
        for h in heads:
            run_ref[h] = runs[h]
            out = parts[0][h]
            for b in nb[1:]:
                out = out + parts[b][h]
            acc_ref[h] += out

    visit([(i, (ci < ri) & (ci + i * SB_BLOCK >= pad))])

    n_mid = jnp.maximum(i - 1, 0)
    odd = n_mid & 1

    @pl.when(odd == 1)
    def _():
        visit([(i - 1, None)])

    def middle(step, carry):
        j = i - 1 - odd - 2 * step
        visit([(j, None), (j - 1, None)])
        return carry

    lax.fori_loop(0, n_mid // 2, middle, 0)

    @pl.when(i > 0)
    def _():
        visit([(0, ci >= pad)])

    for p in range(n_pairs):
        o_ref[0, :, p * pair_lanes:(p + 1) * pair_lanes] = jnp.where(
            ci < HEAD_DIM, acc_ref[2 * p], acc_ref[2 * p + 1]).astype(o_ref.dtype)


def _sb_prompt(q, k, v, bias, pad):
    b, t, width = q.shape
    nq = t // SB_BLOCK
    resident = pl.BlockSpec((1, t, width), lambda bi, i: (bi, 0, 0), pipeline_mode=pl.Buffered(1))
    return pl.pallas_call(
        functools.partial(_sb_prompt_kernel, pad=pad),
        grid=(b, nq),
        in_specs=[pl.BlockSpec(memory_space=pltpu.SMEM),
                  pl.BlockSpec((1, SB_BLOCK, width), lambda bi, i: (bi, i, 0)),
                  resident, resident],
        out_specs=pl.BlockSpec((1, SB_BLOCK, width), lambda bi, i: (bi, i, 0)),
        out_shape=jax.ShapeDtypeStruct((b, t, width), BF16),
        scratch_shapes=[pltpu.VMEM((N_HEADS, SB_BLOCK, SB_BLOCK), BF16),
                        pltpu.VMEM((N_HEADS, SB_BLOCK, SB_BLOCK), F32),
                        pltpu.VMEM((N_HEADS, SB_BLOCK, SB_BLOCK), F32)],
        compiler_params=_params("parallel", "arbitrary"),
        name="sb_prompt",
    )(bias, q, k, v)


def _sb_sample_kernel(pt_ref, q_ref, bias_ref, knew_ref, vnew_ref, *refs, pages_per_step, t_new):
    k_refs = refs[:pages_per_step]
    v_refs = refs[pages_per_step:2 * pages_per_step]
    o_ref = refs[2 * pages_per_step]
    acc_ref, run_ref = refs[2 * pages_per_step + 1:]
    g = pl.program_id(1)
    rows = N_HEADS * t_new
    heads = range(N_HEADS)
    bias = bias_ref[...] * LOG2E
    suffix_total = _suffix_total()

    def logits(kt_of_head):
        return jnp.concatenate([_dot(q_ref[0, h], kt_of_head(h)) for h in heads], axis=0) + bias

    def outputs(w, vt_of_head):
        return [_dot_nt(w[h * t_new:(h + 1) * t_new, :].astype(BF16), vt_of_head(h)) for h in heads]

    def store_state(run, accs):
        run_ref[...] = run
        for h in heads:
            acc_ref[h] = accs[h]

    @pl.when(g == 0)
    def _():
        ri = _iota((rows, LANES), 0)
        ci = _iota((rows, LANES), 1)
        mask = ci < lax.rem(ri, t_new)
        z = logits(lambda h: knew_ref[0, h])
        w, run = _sb_block(z, mask, jnp.zeros((rows, LANES), F32), suffix_total)
        store_state(run, outputs(w, lambda h: vnew_ref[0, h]))

    def page_head(ref):
        return lambda h: ref[0, 0, h].astype(BF16)

    order = list(reversed(range(pages_per_step)))
    zs = [logits(page_head(k_refs[p])) for p in order]
    logs = [_sb_log_parts(z, None) for z in zs]
    sums = [_dot(log_stay, suffix_total) for _, log_stay in logs]
    run = run_ref[...]
    ws = []
    for (log_beta, _), s in zip(logs, sums):
        w, run = _sb_weights((log_beta, s), None, run)
        ws.append(w)
    accs = [acc_ref[h] for h in heads]
    for w, p in zip(ws, order):
        outs = outputs(w, page_head(v_refs[p]))
        accs = [accs[h] + outs[h] for h in heads]
    store_state(run, accs)

    @pl.when(g == pl.num_programs(1) - 1)
    def _():
        for h in heads:
            o_ref[0, :, h * HEAD_DIM:(h + 1) * HEAD_DIM] = acc_ref[h].astype(o_ref.dtype)


def _sb_sample(page_table, q, bias_rows, knew_t, vnew_t, cache_kt, cache_vt, layer, pages_per_step):
    b, _, t_new, _ = q.shape
    rows = N_HEADS * t_new
    n_pages = page_table.shape[1]
    n_groups = n_pages // pages_per_step

    def page_spec(p):
        def index(bi, g, pt):
            return (layer, pt[bi, (n_groups - 1 - g) * pages_per_step + p], 0, 0, 0)
        return pl.BlockSpec((1, 1, N_HEADS, HEAD_DIM, PAGE), index)

    per_batch = lambda r, c: pl.BlockSpec((1, N_HEADS, r, c), lambda bi, g, pt: (bi, 0, 0, 0))
    return pl.pallas_call(
        functools.partial(_sb_sample_kernel, pages_per_step=pages_per_step, t_new=t_new),
        grid_spec=pltpu.PrefetchScalarGridSpec(
            num_scalar_prefetch=1,
            grid=(b, n_groups),
            in_specs=[per_batch(t_new, HEAD_DIM),
                      pl.BlockSpec((rows, LANES), lambda bi, g, pt: (0, 0)),
                      per_batch(HEAD_DIM, PAGE), per_batch(HEAD_DIM, PAGE)]
                     + [page_spec(p) for p in range(pages_per_step)] * 2,
            out_specs=pl.BlockSpec((1, t_new, WIDTH), lambda bi, g, pt: (bi, 0, 0)),
            scratch_shapes=[pltpu.VMEM((N_HEADS, t_new, HEAD_DIM), F32),
                            pltpu.VMEM((rows, LANES), F32)],
        ),
        out_shape=jax.ShapeDtypeStruct((b, t_new, WIDTH), BF16),
        compiler_params=_params("parallel", "arbitrary"),
        name="sb_sample",
    )(page_table, q, bias_rows, knew_t, vnew_t,
      *([cache_kt] * pages_per_step), *([cache_vt] * pages_per_step))


def _outproj_kernel(x_ref, oa_ref, ob_ref, w_ref, y_ref):
    y_ref[...] = (x_ref[...] + _dot(oa_ref[...], w_ref[0:WIDTH, :])
                  + _dot(ob_ref[...], w_ref[WIDTH:2 * WIDTH, :]))


def _outproj(x, oa, ob, w, tile):
    n, d = x.shape
    return pl.pallas_call(
        _outproj_kernel,
        grid=(n // tile,),
        in_specs=[pl.BlockSpec((tile, d), lambda i: (i, 0)),
                  pl.BlockSpec((tile, WIDTH), lambda i: (i, 0)),
                  pl.BlockSpec((tile, WIDTH), lambda i: (i, 0)),
                  pl.BlockSpec((2 * WIDTH, d), lambda i: (0, 0))],
        out_specs=pl.BlockSpec((tile, d), lambda i: (i, 0)),
        out_shape=jax.ShapeDtypeStruct((n, d), F32),
        compiler_params=_params("parallel"),
        name="outproj",
    )(x, oa, ob, w)


def _top_values(x, count):
    vals = []
    for _ in range(count):
        m = jnp.max(x, axis=0, keepdims=True)
        vals.append(m)
        x = jnp.where(x == m, -jnp.inf, x)
    return jnp.concatenate(vals, axis=0)


def _route_kernel(x_ref, lnw_ref, wq_hi_ref, wq_mid_ref, wq_lo_ref, k1_ref, k2_ref,
                  h_ref, e1_ref, thr_ref, e2_ref):
    x = x_ref[...]
    hn = x * lax.rsqrt(jnp.mean(x * x, axis=-1, keepdims=True) + RMS_EPS) * lnw_ref[...]
    h_hi, h_mid, h_lo = _split3(hn)
    h_ref[...] = h_hi
    half = PEER_KEYS

    def query(cols):
        w_hi, w_mid, w_lo = wq_hi_ref[:, cols], wq_mid_ref[:, cols], wq_lo_ref[:, cols]
        return (_dot(h_lo, w_hi) + _dot(h_hi, w_lo) + _dot(h_mid, w_mid)
                + _dot(h_mid, w_hi) + _dot(h_hi, w_mid) + _dot(h_hi, w_hi))

    few = PEER_TOPK // 2

    head_cols = lambda h: slice(2 * h * half, (2 * h + 2) * half)
    q_next = query(head_cols(0))
    for h in range(N_HEADS):
        q = q_next
        if h + 1 < N_HEADS:
            q_next = query(head_cols(h + 1))
        q1, q2 = q[:, :half], q[:, half:]
        s1 = _dot_nt(k1_ref[...], q1, HIGHEST)
        s2 = _dot_nt(k2_ref[...], q2, HIGHEST)
        v1 = _top_values(s1, PEER_TOPK)
        v2 = _top_values(s2, PEER_TOPK)
        top = v1[0:1, :] + v2[0:1, :]
        e2_top = jnp.exp(v2 - v2[0:1, :])
        rows = [(v1[0:1, :] + v2, e2_top)]
        rows += [(v1[i:i + 1, :] + v2[0:few, :], e2_top[0:few, :]) for i in range(1, few)]
        tail = v1[few:, :] + v2[0:1, :]
        kth = _top_values(jnp.concatenate([r[0] for r in rows] + [tail], axis=0),
                          PEER_TOPK)[PEER_TOPK - 1:PEER_TOPK, :]
        z = jnp.zeros_like(top)
        thr = jnp.full_like(s1, jnp.inf)
        for i, (sums, e2_row) in enumerate(rows):
            picked = sums >= kth
            z = z + jnp.sum(jnp.where(picked, jnp.exp(sums - top), 0.0), axis=0, keepdims=True)
            row_thr = jnp.min(jnp.where(picked, e2_row, jnp.inf), axis=0, keepdims=True)
            thr = jnp.where(s1 == v1[i:i + 1, :], row_thr, thr)
        picked = tail >= kth
        z = z + jnp.sum(jnp.where(picked, jnp.exp(tail - top), 0.0), axis=0, keepdims=True)
        tail_thr = jnp.where(picked, e2_top[0:1, :], jnp.inf)
        for r in range(PEER_TOPK - few):
            thr = jnp.where(s1 == v1[few + r:few + r + 1, :], tail_thr[r:r + 1, :], thr)
        e1_ref[h] = jnp.exp(s1 - v1[0:1, :]) / z
        thr_ref[h] = thr
        e2_ref[h] = jnp.exp(s2 - v2[0:1, :])


def _route(x, lnw, wq, k1, k2, tile):
    wq_pieces = _split3(wq)
    wq_spec = pl.BlockSpec(wq.shape, lambda i: (0, 0), pipeline_mode=pl.Buffered(1))
    n, d = x.shape
    per_head = pl.BlockSpec((N_HEADS, PEER_KEYS, tile), lambda i: (0, 0, i))
    head_shape = jax.ShapeDtypeStruct((N_HEADS, PEER_KEYS, n), F32)
    return pl.pallas_call(
        _route_kernel,
        grid=(n // tile,),
        in_specs=[pl.BlockSpec((tile, d), lambda i: (i, 0)),
                  pl.BlockSpec((1, d), lambda i: (0, 0)),
                  wq_spec, wq_spec, wq_spec,
                  pl.BlockSpec(k1.shape, lambda i: (0, 0)),
                  pl.BlockSpec(k2.shape, lambda i: (0, 0))],
        out_specs=[pl.BlockSpec((tile, d), lambda i: (i, 0)), per_head, per_head, per_head],
        out_shape=[jax.ShapeDtypeStruct((n, d), BF16), head_shape, head_shape, head_shape],
        compiler_params=_params("parallel"),
        name="peer_route",
    )(x, lnw, *wq_pieces, k1, k2)


def _gelu_tanh(x):
    c = 0.7978845608028654
    return 0.5 * x * (1.0 + jnp.tanh(c * (x + 0.044715 * (x * x * x))))


def _peer_kernel(x_ref, h_ref, u_ref, vt_ref, e1_ref, thr_ref, e2_ref, y_ref, acc_ref, w_ref,
                 *, expert_tile):
    j = pl.program_id(1)
    tile = h_ref.shape[0]
    groups = expert_tile // PEER_KEYS

    @pl.when(j == 0)
    def _():
        acc_ref[...] = jnp.zeros_like(acc_ref)

    act = _gelu_tanh(_dot_nt(u_ref[...], h_ref[...]))
    for a_local in range(groups):
        a = j * groups + a_local
        gate = jnp.zeros((PEER_KEYS, tile), F32)
        for h in range(N_HEADS):
            e2 = e2_ref[h]
            gate = gate + jnp.where(e2 >= thr_ref[h, pl.ds(a, 1), :], e2, 0.0) * e1_ref[h, pl.ds(a, 1), :]
        rows = slice(a_local * PEER_KEYS, (a_local + 1) * PEER_KEYS)
        w_ref[rows, :] = (gate * act[rows, :]).astype(BF16)
    acc_ref[...] += _dot(vt_ref[...], w_ref[...])

    @pl.when(j == pl.num_programs(1) - 1)
    def _():
        y_ref[...] = x_ref[...] + acc_ref[...].T


def _peer(x, hn, u, vt, e1, thr, e2, tile, expert_tile):
    n, d = x.shape
    per_head = pl.BlockSpec((N_HEADS, PEER_KEYS, tile), lambda i, j: (0, 0, i))
    return pl.pallas_call(
        functools.partial(_peer_kernel, expert_tile=expert_tile),
        grid=(n // tile, u.shape[0] // expert_tile),
        in_specs=[pl.BlockSpec((tile, d), lambda i, j: (i, 0)),
                  pl.BlockSpec((tile, d), lambda i, j: (i, 0)),
                  pl.BlockSpec((expert_tile, d), lambda i, j: (j, 0)),
                  pl.BlockSpec((d, expert_tile), lambda i, j: (0, j)),
                  per_head, per_head, per_head],
        out_specs=pl.BlockSpec((tile, d), lambda i, j: (i, 0)),
        out_shape=jax.ShapeDtypeStruct((n, d), F32),
        scratch_shapes=[pltpu.VMEM((d, tile), F32), pltpu.VMEM((expert_tile, tile), BF16)],
        compiler_params=_params("parallel", "arbitrary"),
        name="peer_dense",
    )(x, hn, u, vt, e1, thr, e2)


def _norm_kernel(x_ref, w_ref, y_ref):
    x = x_ref[...]
    y_ref[...] = x * lax.rsqrt(jnp.mean(x * x, axis=-1, keepdims=True) + RMS_EPS) * w_ref[...]


def _final_norm(x, w, tile):
    n, d = x.shape
    return pl.pallas_call(
        _norm_kernel,
        grid=(n // tile,),
        in_specs=[pl.BlockSpec((tile, d), lambda i: (i, 0)), pl.BlockSpec((1, d), lambda i: (0, 0))],
        out_specs=pl.BlockSpec((tile, d), lambda i: (i, 0)),
        out_shape=jax.ShapeDtypeStruct((n, d), F32),
        compiler_params=_params("parallel"),
        name="final_norm",
    )(x, w)


ROW_TILE = 256
PEER_EXPERT_TILE = 1024
PAGES_PER_STEP = 8


def _heads_major(a):
    b, t, _ = a.shape
    return a.reshape(b, t, N_HEADS, HEAD_DIM).transpose(0, 2, 1, 3)


def _pad_lanes(a, width):
    return jnp.pad(a, (0, width - a.shape[0]))[None]


def kernel(x_prompt, x_sample, cache_sb_k, cache_sb_v, page_table, state_gdn, state_conv,
           meta_tokens, ln1_w, ln2_w, lnf_w, w_in, conv_w, gdn_a_log, gdn_dt_bias, gdn_norm_w,
           sb_logit_bias, w_out, peer_wq, peer_k1, peer_k2, peer_u, peer_v):
    bp, seq, d = x_prompt.shape
    bs, ts, _ = x_sample.shape
    depth = w_in.shape[0]
    pad = (-N_META) % SB_BLOCK
    tp = pad + N_META + seq
    np_rows = bp * tp
    ns_rows = bs * ts
    hist = CONV_TAPS - 1

    meta = jnp.broadcast_to(meta_tokens[None], (bp, N_META, d))
    xp = jnp.concatenate([jnp.zeros((bp, pad, d), F32), meta, x_prompt], axis=1)
    x = jnp.concatenate([xp.reshape(np_rows, d), x_sample.reshape(ns_rows, d)], axis=0)

    cache_kt = cache_sb_k.transpose(0, 1, 3, 4, 2)
    cache_vt = cache_sb_v.transpose(0, 1, 3, 4, 2)
    zero_state = jnp.zeros((bp, N_HEADS, HEAD_DIM, HEAD_DIM), F32)
    zero_conv = jnp.zeros((bp, hist, QKV_WIDTH), F32)

    outs = {k: [] for k in ("kp", "vp", "ks", "vs", "sp", "ss", "cp", "cs")}
    for l in range(depth):
        wl = w_in[l]
        o_gate = QKV_WIDTH
        o_a = o_gate + WIDTH
        o_qb = o_a + 2 * N_HEADS
        w_perm = jnp.concatenate(
            [wl[:, :o_a], wl[:, o_qb:], wl[:, o_a:o_qb],
             jnp.zeros((d, LANES - 2 * N_HEADS), F32)], axis=1).astype(BF16)
        qkv, gate, ab, qb, kb, vb, kb16, vb16 = _inproj(x, ln1_w[l][None], w_perm, ROW_TILE)

        gdn_consts = (conv_w[l], _pad_lanes(gdn_a_log[l], LANES), _pad_lanes(gdn_dt_bias[l], LANES),
                      jnp.tile(gdn_norm_w[l], QUAD)[None])
        split = lambda a, w_: (a[:np_rows].reshape(bp, tp, w_), a[np_rows:].reshape(bs, ts, w_))
        qkv_p, qkv_s = split(qkv, QKV_WIDTH)
        ab_p, ab_s = split(ab, LANES)
        gate_p, gate_s = split(gate, WIDTH)
        oa_p, s_p, c_p = _gdn(qkv_p, ab_p, gate_p, *gdn_consts, zero_state, zero_conv, GDN_CHUNK)
        oa_s, s_s, c_s = _gdn(qkv_s, ab_s, gate_s, *gdn_consts, state_gdn[l], state_conv[l], ts)

        qb_p, qb_s = split(qb, WIDTH)
        kb16_p, kb16_s = split(kb16, WIDTH)
        vb16_p, vb16_s = split(vb16, WIDTH)
        ob_p = _sb_prompt(qb_p, kb16_p, vb16_p, sb_logit_bias[l], pad).reshape(np_rows, WIDTH)

        bias_rows = jnp.broadcast_to(jnp.repeat(sb_logit_bias[l], ts)[:, None], (N_HEADS * ts, LANES))
        grow_t = lambda a: jnp.pad(_heads_major(a).transpose(0, 1, 3, 2), ((0, 0), (0, 0), (0, 0), (0, PAGE - ts)))
        ob_s = _sb_sample(page_table, _heads_major(qb_s), bias_rows, grow_t(kb16_s), grow_t(vb16_s),
                          cache_kt, cache_vt, l, PAGES_PER_STEP).reshape(ns_rows, WIDTH)

        oa = jnp.concatenate([oa_p.reshape(np_rows, WIDTH), oa_s.reshape(ns_rows, WIDTH)], axis=0)
        ob = jnp.concatenate([ob_p, ob_s], axis=0)
        x = _outproj(x, oa, ob, w_out[l].astype(BF16), ROW_TILE)

        hn, e1, thr, e2 = _route(x, ln2_w[l][None], peer_wq[l], peer_k1[l], peer_k2[l], ROW_TILE)
        x = _peer(x, hn, peer_u[l].astype(BF16), peer_v[l].astype(BF16).T, e1, thr, e2,
                  ROW_TILE, PEER_EXPERT_TILE)

        kb_p, kb_s = split(kb, WIDTH)
        vb_p, vb_s = split(vb, WIDTH)
        outs["kp"].append(kb_p[:, pad:].reshape(bp, tp - pad, N_HEADS, HEAD_DIM))
        outs["vp"].append(vb_p[:, pad:].reshape(bp, tp - pad, N_HEADS, HEAD_DIM))
        outs["ks"].append(kb_s.reshape(bs, ts, N_HEADS, HEAD_DIM))
        outs["vs"].append(vb_s.reshape(bs, ts, N_HEADS, HEAD_DIM))
        outs["sp"].append(s_p); outs["ss"].append(s_s)
        outs["cp"].append(c_p); outs["cs"].append(c_s)

    y = _final_norm(x, lnf_w[None], ROW_TILE)
    y_prompt = y[:np_rows].reshape(bp, tp, d)[:, pad + N_META:]
    y_sample = y[np_rows:].reshape(bs, ts, d)
    stack = lambda key: jnp.stack(outs[key])
    return (y_prompt, y_sample, stack("kp"), stack("vp"), stack("ks"), stack("vs"),
            stack("sp"), stack("ss"), stack("cp"), stack("cs"))
```

```python
import functools

import jax
import jax.numpy as jnp
from jax import lax
from jax.experimental import pallas as pl
from jax.experimental.pallas import tpu as pltpu

F32 = jnp.float32
BF16 = jnp.bfloat16
HIGHEST = lax.Precision.HIGHEST

HEAD_DIM = 64
N_HEADS = 8
WIDTH = N_HEADS * HEAD_DIM
QKV_WIDTH = 3 * WIDTH
QUAD = 4
QUAD_WIDTH = QUAD * HEAD_DIM
CONV_TAPS = 4
N_META = 16
GDN_CHUNK = 64
SB_BLOCK = 128
PAGE = 128
PEER_KEYS = 128
PEER_TOPK = 16
RMS_EPS = 1e-6
L2_EPS = 1e-6
LOG2E = 1.4426950408889634
LANES = 128
VMEM_LIMIT = 56 * 1024 * 1024


def _dot(a, b, precision=None):
    return jnp.dot(a, b, preferred_element_type=F32, precision=precision)


def _dot_nt(a, b, precision=None):
    return lax.dot_general(a, b, (((1,), (1,)), ((), ())),
                           preferred_element_type=F32, precision=precision)


def _mm(a, b):
    return _dot(a.astype(BF16), b.astype(BF16))


def _mm_nt(a, b):
    return _dot_nt(a.astype(BF16), b.astype(BF16))


def _split3(x):
    hi = x.astype(BF16)
    rest = x - hi.astype(F32)
    mid = rest.astype(BF16)
    lo = (rest - mid.astype(F32)).astype(BF16)
    return hi, mid, lo


def _params(*sem):
    return pltpu.CompilerParams(dimension_semantics=sem, vmem_limit_bytes=VMEM_LIMIT)


def _iota(shape, axis):
    return lax.broadcasted_iota(jnp.int32, shape, axis)


def _log2_int(n):
    assert n & (n - 1) == 0, n
    return n.bit_length() - 1


def _inproj_kernel(x_ref, lnw_ref, w_ref, qkv_ref, gate_ref, ab_ref, qb_ref,
                   kb_ref, vb_ref, kb16_ref, vb16_ref):
    x = x_ref[...]
    h = x * lax.rsqrt(jnp.mean(x * x, axis=-1, keepdims=True) + RMS_EPS) * lnw_ref[...]
    h = h.astype(BF16)
    o = 0
    qkv_ref[...] = _dot(h, w_ref[:, o:o + QKV_WIDTH]); o += QKV_WIDTH
    gate_ref[...] = _dot(h, w_ref[:, o:o + WIDTH]); o += WIDTH
    qb_ref[...] = (_dot(h, w_ref[:, o:o + WIDTH]) * (LOG2E * HEAD_DIM ** -0.5)).astype(BF16); o += WIDTH
    kb = _dot(h, w_ref[:, o:o + WIDTH]); o += WIDTH
    kb_ref[...] = kb
    kb16_ref[...] = kb.astype(BF16)
    vb = _dot(h, w_ref[:, o:o + WIDTH]); o += WIDTH
    vb_ref[...] = vb
    vb16_ref[...] = vb.astype(BF16)
    ab_ref[...] = _dot(h, w_ref[:, o:o + LANES])


def _inproj(x, lnw, w, tile):
    n, d = x.shape
    wcols = w.shape[1]
    row = lambda width: pl.BlockSpec((tile, width), lambda i: (i, 0))
    shp = lambda width, dt: jax.ShapeDtypeStruct((n, width), dt)
    return pl.pallas_call(
        _inproj_kernel,
        grid=(n // tile,),
        in_specs=[row(d), pl.BlockSpec((1, d), lambda i: (0, 0)),
                  pl.BlockSpec((d, wcols), lambda i: (0, 0))],
        out_specs=[row(QKV_WIDTH), row(WIDTH), row(LANES), row(WIDTH), row(WIDTH),
                   row(WIDTH), row(WIDTH), row(WIDTH)],
        out_shape=[shp(QKV_WIDTH, F32), shp(WIDTH, F32), shp(LANES, F32), shp(WIDTH, BF16),
                   shp(WIDTH, F32), shp(WIDTH, F32), shp(WIDTH, BF16), shp(WIDTH, BF16)],
        compiler_params=_params("parallel"),
        name="inproj",
    )(x, lnw, w)


def _silu(x):
    return x * (1.0 / (1.0 + jnp.exp(-x)))


def _softplus(x):
    return jnp.maximum(x, 0.0) + jnp.log1p(jnp.exp(-jnp.abs(x)))


def _dot_sel(x, sel):
    hi, mid, lo = _split3(x)
    return _dot(hi, sel) + _dot(mid, sel) + _dot(lo, sel)


def _unit_lower_inverses(lows, rows, block):
    ri = _iota((rows, rows), 0)
    ci = _iota((rows, rows), 1)
    invs = [(ri == ci).astype(F32)] * len(lows)
    s = 1
    while s < block:
        off = ((ri & s) != 0) & ((ci & s) == 0) & ((ri ^ ci) < 2 * s)
        blks = [jnp.where(off, low, 0.0).astype(BF16) for low in lows]
        inv16 = [inv.astype(BF16) for inv in invs]
        right = [_dot(blk, i16) for blk, i16 in zip(blks, inv16)]
        invs = [inv - _dot(i16, r.astype(BF16)) for inv, i16, r in zip(invs, inv16, right)]
        s *= 2
    return invs


def _gdn_kernel(qkv_ref, ab_ref, gate_ref, convw_ref, alog_ref, dtb_ref, normw_ref,
                s0_ref, conv0_ref, o_ref, sout_ref, convout_ref, xbuf, s_scr, *, chunk):
    c = pl.program_id(1)
    hist = CONV_TAPS - 1
    rows = QUAD * chunk
    chunk_shift = _log2_int(chunk)
    head_shift = _log2_int(HEAD_DIM)
    n_quads = N_HEADS // QUAD

    @pl.when(c == 0)
    def _():
        xbuf[0:8, :] = jnp.zeros((8, QKV_WIDTH), F32)
        xbuf[8 - hist:8, :] = conv0_ref[0]
        s_scr[...] = jnp.zeros_like(s_scr)
        for h in range(N_HEADS):
            blk = slice((h % QUAD) * HEAD_DIM, (h % QUAD + 1) * HEAD_DIM)
            s_scr[h // QUAD, blk, blk] = s0_ref[0, h]

    xbuf[8:8 + chunk, :] = qkv_ref[0]
    conv = convw_ref[0:1, :] * xbuf[8 - hist:8 - hist + chunk, :]
    for i in range(1, CONV_TAPS):
        conv = conv + convw_ref[i:i + 1, :] * xbuf[8 - hist + i:8 - hist + i + chunk, :]
    convout_ref[0] = xbuf[8 + chunk - hist:8 + chunk, :]
    tail = xbuf[chunk:chunk + 8, :]
    xbuf[0:8, :] = tail
    act = _silu(conv)

    ab = ab_ref[0]
    lane = _iota((chunk, LANES), 1)
    g = -jnp.exp(alog_ref[...]) * _softplus(ab + dtb_ref[...])
    beta = 1.0 / (1.0 + jnp.exp(-ab))
    tril = (_iota((chunk, chunk), 0) >= _iota((chunk, chunk), 1)).astype(BF16)
    g_hi, g_mid, g_lo = _split3(g)
    gcum = _dot(tril, g_hi) + _dot(tril, g_mid) + _dot(tril, g_lo)
    eg_beta = jnp.where(lane < N_HEADS, jnp.exp(gcum), beta)
    edec = jnp.exp(gcum[chunk - 1:chunk, :] - gcum)

    lane_head = lax.shift_right_logical(_iota((LANES, QUAD_WIDTH), 1), head_shift)
    src_lane = _iota((LANES, QUAD_WIDTH), 0)
    block_ones = (lax.shift_right_logical(_iota((QUAD_WIDTH, QUAD_WIDTH), 0), head_shift)
                  == lax.shift_right_logical(_iota((QUAD_WIDTH, QUAD_WIDTH), 1), head_shift)).astype(BF16)
    ones_k = jnp.ones((LANES, rows), BF16)
    ri = _iota((rows, rows), 0)
    ci = _iota((rows, rows), 1)
    same = lax.shift_right_logical(ri, chunk_shift) == lax.shift_right_logical(ci, chunk_shift)
    incl = same & (ri >= ci)
    strict = same & (ri > ci)
    exp_row_head = lax.shift_right_logical(_iota((rows, QUAD_WIDTH), 0), chunk_shift)
    exp_lane_head = lax.shift_right_logical(_iota((rows, QUAD_WIDTH), 1), head_shift)
    own_block = exp_row_head == exp_lane_head
    stack_row_head = lax.shift_right_logical(_iota((rows, LANES), 0), chunk_shift)
    stack_lane = _iota((rows, LANES), 1)

    def expand(x):
        return jnp.where(own_block, jnp.concatenate([x] * QUAD, axis=0), 0.0)

    quads = range(n_quads)
    ones_r = jnp.ones((rows, LANES), BF16)

    def prepare(quad):
        first = quad * QUAD
        sel_g = (src_lane == lane_head + first).astype(BF16)
        sel_b = (src_lane == lane_head + (N_HEADS + first)).astype(BF16)
        eg_w = _dot_sel(eg_beta, sel_g)
        beta_w = _dot_sel(eg_beta, sel_b)
        edec_w = _dot_sel(edec, sel_g)
        q = act[:, quad * QUAD_WIDTH:(quad + 1) * QUAD_WIDTH]
        k = act[:, WIDTH + quad * QUAD_WIDTH:WIDTH + (quad + 1) * QUAD_WIDTH]
        v = act[:, 2 * WIDTH + quad * QUAD_WIDTH:2 * WIDTH + (quad + 1) * QUAD_WIDTH]
        q = q * lax.rsqrt(_dot_sel(q * q, block_ones) + L2_EPS) * (HEAD_DIM ** -0.5)
        k = k * lax.rsqrt(_dot_sel(k * k, block_ones) + L2_EPS)
        kbeta = k * beta_w
        stacked = jnp.where(stack_lane == stack_row_head + first,
                            jnp.concatenate([gcum] * QUAD, axis=0), 0.0)
        s_hi, s_mid, s_lo = _split3(stacked)
        g_row = _dot(s_hi, ones_k) + _dot(s_mid, ones_k) + _dot(s_lo, ones_k)
        g_col = _dot_nt(ones_r, s_hi) + _dot_nt(ones_r, s_mid) + _dot_nt(ones_r, s_lo)
        decay = jnp.where(incl, jnp.exp(g_row - g_col), 0.0)
        e_k = expand(k).astype(BF16)
        low = jnp.where(strict, _dot_nt(expand(kbeta).astype(BF16), e_k) * decay, 0.0)
        return dict(decay=decay, e_k=e_k, low=low, eg_last=eg_w[chunk - 1:chunk, :],
                    e_q=expand(q).astype(BF16), e_qg=expand(q * eg_w).astype(BF16),
                    e_vb=expand(v * beta_w).astype(BF16), e_kg=expand(kbeta * eg_w).astype(BF16),
                    e_kd=expand(k * edec_w))

    pre = [prepare(quad) for quad in quads]
    invs = _unit_lower_inverses([p["low"] for p in pre], rows, chunk)
    invs = [inv.astype(BF16) for inv in invs]
    states = [s_scr[quad] for quad in quads]
    states16 = [st.astype(BF16) for st in states]
    sol_v = [_dot(invs[i], pre[i]["e_vb"]) for i in quads]
    sol_k = [_dot(invs[i], pre[i]["e_kg"]) for i in quads]
    attn = [_dot_nt(pre[i]["e_q"], pre[i]["e_k"]) * pre[i]["decay"] for i in quads]
    from_state = [_dot(pre[i]["e_qg"], states16[i]) for i in quads]
    u = [sol_v[i] - _mm(sol_k[i], states16[i]) for i in quads]
    e_o = [from_state[i] + _mm(attn[i], u[i]) for i in quads]
    for i in quads:
        s_scr[i] = states[i] * pre[i]["eg_last"] + lax.dot_general(
            pre[i]["e_kd"], u[i], (((0,), (0,)), ((), ())), preferred_element_type=F32)
    for i in quads:
        o = e_o[i][0:chunk, :]
        for h in range(1, QUAD):
            o = o + e_o[i][h * chunk:(h + 1) * chunk, :]
        on = o * lax.rsqrt(_dot_sel(o * o, block_ones) * (1.0 / HEAD_DIM) + RMS_EPS)
        cols = slice(i * QUAD_WIDTH, (i + 1) * QUAD_WIDTH)
        o_ref[0, :, cols] = (on * normw_ref[...] * _silu(gate_ref[0, :, cols])).astype(o_ref.dtype)

    @pl.when(c == pl.num_programs(1) - 1)
    def _():
        for h in range(N_HEADS):
            blk = slice((h % QUAD) * HEAD_DIM, (h % QUAD + 1) * HEAD_DIM)
            sout_ref[0, h] = s_scr[h // QUAD, blk, blk]


def _gdn(qkv, ab, gate, convw, alog, dtb, normw, s0, conv0, chunk):
    b, t, _ = qkv.shape
    n = t // chunk
    tok = lambda width: pl.BlockSpec((1, chunk, width), lambda i, c: (i, c, 0))
    const2 = lambda r, cdim: pl.BlockSpec((r, cdim), lambda i, c: (0, 0))
    hist = CONV_TAPS - 1
    state_spec = pl.BlockSpec((1, N_HEADS, HEAD_DIM, HEAD_DIM), lambda i, c: (i, 0, 0, 0))
    conv_spec = pl.BlockSpec((1, hist, QKV_WIDTH), lambda i, c: (i, 0, 0))
    return pl.pallas_call(
        functools.partial(_gdn_kernel, chunk=chunk),
        grid=(b, n),
        in_specs=[tok(QKV_WIDTH), tok(LANES), tok(WIDTH),
                  const2(CONV_TAPS, QKV_WIDTH), const2(1, LANES), const2(1, LANES),
                  const2(1, QUAD_WIDTH), state_spec, conv_spec],
        out_specs=[tok(WIDTH), state_spec, conv_spec],
        out_shape=[jax.ShapeDtypeStruct((b, t, WIDTH), BF16),
                   jax.ShapeDtypeStruct((b, N_HEADS, HEAD_DIM, HEAD_DIM), F32),
                   jax.ShapeDtypeStruct((b, hist, QKV_WIDTH), F32)],
        scratch_shapes=[pltpu.VMEM((chunk + 8, QKV_WIDTH), F32),
                        pltpu.VMEM((N_HEADS // QUAD, QUAD_WIDTH, QUAD_WIDTH), F32)],
        compiler_params=_params("parallel", "arbitrary"),
        name=f"gdn_c{chunk}",
    )(qkv, ab, gate, convw, alog, dtb, normw, s0, conv0)


def _suffix_total():
    r = _iota((SB_BLOCK, 2 * SB_BLOCK), 0)
    c = _iota((SB_BLOCK, 2 * SB_BLOCK), 1)
    return ((r > c) | (c >= SB_BLOCK)).astype(BF16)


def _sb_log_parts(z, mask):
    sp = jnp.log2(1.0 + jnp.exp2(-jnp.abs(z)))
    log_beta = jnp.minimum(z, 0.0) - sp
    log_stay = log_beta - z
    if mask is not None:
        log_stay = jnp.where(mask, log_stay, 0.0)
    return log_beta, log_stay.astype(BF16)


def _sb_logs(z, mask, suffix_total):
    log_beta, log_stay = _sb_log_parts(z, mask)
    return log_beta, _dot(log_stay, suffix_total)


def _sb_weights(logs, mask, run):
    log_beta, sums = logs
    w = jnp.exp2(log_beta + run + sums[:, :SB_BLOCK])
    if mask is not None:
        w = jnp.where(mask, w, 0.0)
    return w, run + sums[:, SB_BLOCK:]


def _sb_block(z, mask, run, suffix_total):
    return _sb_weights(_sb_logs(z, mask, suffix_total), mask, run)


def _sb_prompt_kernel(bias_ref, q_ref, k_ref, v_ref, o_ref, qm_ref, acc_ref, run_ref, *, pad):
    i = pl.program_id(1)
    pair_lanes = 2 * HEAD_DIM
    n_pairs = N_HEADS // 2
    ri = _iota((SB_BLOCK, SB_BLOCK), 0)
    ci = _iota((SB_BLOCK, SB_BLOCK), 1)
    suffix_total = _suffix_total()

    for p in range(n_pairs):
        qp = q_ref[0, :, p * pair_lanes:(p + 1) * pair_lanes]
        qm_ref[2 * p] = jnp.where(ci < HEAD_DIM, qp, jnp.zeros_like(qp))
        qm_ref[2 * p + 1] = jnp.where(ci >= HEAD_DIM, qp, jnp.zeros_like(qp))
    acc_ref[...] = jnp.zeros_like(acc_ref)
    run_ref[...] = jnp.zeros_like(run_ref)

    def visit(blocks):
        kv = []
        for j, _ in blocks:
            start = pl.multiple_of(j * SB_BLOCK, SB_BLOCK)
            kv.append([(k_ref[0, pl.ds(start, SB_BLOCK), p * pair_lanes:(p + 1) * pair_lanes],
                        v_ref[0, pl.ds(start, SB_BLOCK), p * pair_lanes:(p + 1) * pair_lanes])
                       for p in range(n_pairs)])
        heads = range(N_HEADS)
        nb = range(len(blocks))
        runs = [run_ref[h] for h in heads]
        zs = [[_dot_nt(qm_ref[h], kv[b][h // 2][0]) + bias_ref[h] * LOG2E for h in heads] for b in nb]
        logs = [[_sb_log_parts(zs[b][h], blocks[b][1]) for h in heads] for b in nb]
        sums = [[_dot(logs[b][h][1], suffix_total) for h in heads] for b in nb]
        ws = []
        for b in nb:
            ws.append([])
            for h in heads:
                w, runs[h] = _sb_weights((logs[b][h][0], sums[b][h]), blocks[b][1], runs[h])
                ws[b].append(w.astype(BF16))
        parts = [[_dot(ws[b][h], kv[b][h // 2][1]) for h in heads] for b in nb]
        for h in heads:
            run_ref[h] = runs[h]
            out = parts[0][h]
            for b in nb[1:]:
                out = out + parts[b][h]
            acc_ref[h] += out

    visit([(i, (ci < ri) & (ci + i * SB_BLOCK >= pad))])

    n_mid = jnp.maximum(i - 1, 0)
    odd = n_mid & 1

    @pl.when(odd == 1)
    def _():
        visit([(i - 1, None)])

    def middle(step, carry):
        j = i - 1 - odd - 2 * step
        visit([(j, None), (j - 1, None)])
        return carry

    lax.fori_loop(0, n_mid // 2, middle, 0)

    @pl.when(i > 0)
    def _():
        visit([(0, ci >= pad)])

    for p in range(n_pairs):
        o_ref[0, :, p * pair_lanes:(p + 1) * pair_lanes] = jnp.where(
            ci < HEAD_DIM, acc_ref[2 * p], acc_ref[2 * p + 1]).astype(o_ref.dtype)


def _sb_prompt(q, k, v, bias, pad):
    b, t, width = q.shape
    nq = t // SB_BLOCK
    resident = pl.BlockSpec((1, t, width), lambda bi, i: (bi, 0, 0), pipeline_mode=pl.Buffered(1))
    return pl.pallas_call(
        functools.partial(_sb_prompt_kernel, pad=pad),
        grid=(b, nq),
        in_specs=[pl.BlockSpec(memory_space=pltpu.SMEM),
                  pl.BlockSpec((1, SB_BLOCK, width), lambda bi, i: (bi, i, 0)),
                  resident, resident],
        out_specs=pl.BlockSpec((1, SB_BLOCK, width), lambda bi, i: (bi, i, 0)),
        out_shape=jax.ShapeDtypeStruct((b, t, width), BF16),
        scratch_shapes=[pltpu.VMEM((N_HEADS, SB_BLOCK, SB_BLOCK), BF16),
                        pltpu.VMEM((N_HEADS, SB_BLOCK, SB_BLOCK), F32),
                        pltpu.VMEM((N_HEADS, SB_BLOCK, SB_BLOCK), F32)],
        compiler_params=_params("parallel", "arbitrary"),
        name="sb_prompt",
    )(bias, q, k, v)


def _sb_sample_kernel(pt_ref, q_ref, bias_ref, knew_ref, vnew_ref, *refs, pages_per_step, t_new):
    k_refs = refs[:pages_per_step]
    v_refs = refs[pages_per_step:2 * pages_per_step]
    o_ref = refs[2 * pages_per_step]
    acc_ref, run_ref = refs[2 * pages_per_step + 1:]
    g = pl.program_id(1)
    rows = N_HEADS * t_new
    heads = range(N_HEADS)
    bias = bias_ref[...] * LOG2E
    suffix_total = _suffix_total()

    def logits(kt_of_head):
        return jnp.concatenate([_dot(q_ref[0, h], kt_of_head(h)) for h in heads], axis=0) + bias

    def outputs(w, vt_of_head):
        return [_dot_nt(w[h * t_new:(h + 1) * t_new, :].astype(BF16), vt_of_head(h)) for h in heads]

    def store_state(run, accs):
        run_ref[...] = run
        for h in heads:
            acc_ref[h] = accs[h]

    @pl.when(g == 0)
    def _():
        ri = _iota((rows, LANES), 0)
        ci = _iota((rows, LANES), 1)
        mask = ci < lax.rem(ri, t_new)
        z = logits(lambda h: knew_ref[0, h])
        w, run = _sb_block(z, mask, jnp.zeros((rows, LANES), F32), suffix_total)
        store_state(run, outputs(w, lambda h: vnew_ref[0, h]))

    def page_head(ref):
        return lambda h: ref[0, 0, h].astype(BF16)

    order = list(reversed(range(pages_per_step)))
    zs = [logits(page_head(k_refs[p])) for p in order]
    logs = [_sb_log_parts(z, None) for z in zs]
    sums = [_dot(log_stay, suffix_total) for _, log_stay in logs]
    run = run_ref[...]
    ws = []
    for (log_beta, _), s in zip(logs, sums):
        w, run = _sb_weights((log_beta, s), None, run)
        ws.append(w)
    accs = [acc_ref[h] for h in heads]
    for w, p in zip(ws, order):
        outs = outputs(w, page_head(v_refs[p]))
        accs = [accs[h] + outs[h] for h in heads]
    store_state(run, accs)

    @pl.when(g == pl.num_programs(1) - 1)
    def _():
        for h in heads:
            o_ref[0, :, h * HEAD_DIM:(h + 1) * HEAD_DIM] = acc_ref[h].astype(o_ref.dtype)


def _sb_sample(page_table, q, bias_rows, knew_t, vnew_t, cache_kt, cache_vt, layer, pages_per_step):
    b, _, t_new, _ = q.shape
    rows = N_HEADS * t_new
    n_pages = page_table.shape[1]
    n_groups = n_pages // pages_per_step

    def page_spec(p):
        def index(bi, g, pt):
            return (layer, pt[bi, (n_groups - 1 - g) * pages_per_step + p], 0, 0, 0)
        return pl.BlockSpec((1, 1, N_HEADS, HEAD_DIM, PAGE), index)

    per_batch = lambda r, c: pl.BlockSpec((1, N_HEADS, r, c), lambda bi, g, pt: (bi, 0, 0, 0))
    return pl.pallas_call(
        functools.partial(_sb_sample_kernel, pages_per_step=pages_per_step, t_new=t_new),
        grid_spec=pltpu.PrefetchScalarGridSpec(
            num_scalar_prefetch=1,
            grid=(b, n_groups),
            in_specs=[per_batch(t_new, HEAD_DIM),
                      pl.BlockSpec((rows, LANES), lambda bi, g, pt: (0, 0)),
                      per_batch(HEAD_DIM, PAGE), per_batch(HEAD_DIM, PAGE)]
                     + [page_spec(p) for p in range(pages_per_step)] * 2,
            out_specs=pl.BlockSpec((1, t_new, WIDTH), lambda bi, g, pt: (bi, 0, 0)),
            scratch_shapes=[pltpu.VMEM((N_HEADS, t_new, HEAD_DIM), F32),
                            pltpu.VMEM((rows, LANES), F32)],
        ),
        out_shape=jax.ShapeDtypeStruct((b, t_new, WIDTH), BF16),
        compiler_params=_params("parallel", "arbitrary"),
        name="sb_sample",
    )(page_table, q, bias_rows, knew_t, vnew_t,
      *([cache_kt] * pages_per_step), *([cache_vt] * pages_per_step))


def _outproj_kernel(x_ref, oa_ref, ob_ref, w_ref, y_ref):
    y_ref[...] = (x_ref[...] + _dot(oa_ref[...], w_ref[0:WIDTH, :])
                  + _dot(ob_ref[...], w_ref[WIDTH:2 * WIDTH, :]))


def _outproj(x, oa, ob, w, tile):
    n, d = x.shape
    return pl.pallas_call(
        _outproj_kernel,
        grid=(n // tile,),
        in_specs=[pl.BlockSpec((tile, d), lambda i: (i, 0)),
                  pl.BlockSpec((tile, WIDTH), lambda i: (i, 0)),
                  pl.BlockSpec((tile, WIDTH), lambda i: (i, 0)),
                  pl.BlockSpec((2 * WIDTH, d), lambda i: (0, 0))],
        out_specs=pl.BlockSpec((tile, d), lambda i: (i, 0)),
        out_shape=jax.ShapeDtypeStruct((n, d), F32),
        compiler_params=_params("parallel"),
        name="outproj",
    )(x, oa, ob, w)


def _top_values(x, count, with_rank=False):
    vals = []
    rank = jnp.full_like(x, float(count))
    for r in range(count):
        m = jnp.max(x, axis=0, keepdims=True)
        vals.append(m)
        hit = x == m
        if with_rank:
            rank = jnp.where(hit, float(r), rank)
        x = jnp.where(hit, -jnp.inf, x)
    vals = jnp.concatenate(vals, axis=0)
    return (vals, rank) if with_rank else vals


def _route_kernel(x_ref, lnw_ref, wq_hi_ref, wq_mid_ref, wq_lo_ref, k1_ref, k2_ref,
                  h_ref, e1_ref, cnt_ref, rank_ref, e2_ref):
    x = x_ref[...]
    hn = x * lax.rsqrt(jnp.mean(x * x, axis=-1, keepdims=True) + RMS_EPS) * lnw_ref[...]
    h_hi, h_mid, h_lo = _split3(hn)
    h_ref[...] = h_hi
    half = PEER_KEYS

    def query(cols):
        w_hi, w_mid, w_lo = wq_hi_ref[:, cols], wq_mid_ref[:, cols], wq_lo_ref[:, cols]
        return (_dot(h_lo, w_hi) + _dot(h_hi, w_lo) + _dot(h_mid, w_mid)
                + _dot(h_mid, w_hi) + _dot(h_hi, w_mid) + _dot(h_hi, w_hi))

    few = PEER_TOPK // 2

    head_cols = lambda h: slice(2 * h * half, (2 * h + 2) * half)
    q_next = query(head_cols(0))
    for h in range(N_HEADS):
        q = q_next
        if h + 1 < N_HEADS:
            q_next = query(head_cols(h + 1))
        q1, q2 = q[:, :half], q[:, half:]
        s1 = _dot_nt(k1_ref[...], q1, HIGHEST)
        s2 = _dot_nt(k2_ref[...], q2, HIGHEST)
        v1 = _top_values(s1, PEER_TOPK)
        v2, rank2 = _top_values(s2, PEER_TOPK, with_rank=True)
        top = v1[0:1, :] + v2[0:1, :]
        rows = [v1[0:1, :] + v2] + [v1[i:i + 1, :] + v2[0:few, :] for i in range(1, few)]
        tail = v1[few:, :] + v2[0:1, :]
        kth = _top_values(jnp.concatenate(rows + [tail], axis=0),
                          PEER_TOPK)[PEER_TOPK - 1:PEER_TOPK, :]
        z = jnp.zeros_like(top)
        cnt = jnp.zeros_like(s1)
        for i, sums in enumerate(rows):
            picked = sums >= kth
            z = z + jnp.sum(jnp.where(picked, jnp.exp(sums - top), 0.0), axis=0, keepdims=True)
            n_picked = jnp.sum(jnp.where(picked, 1.0, 0.0), axis=0, keepdims=True)
            cnt = jnp.where(s1 == v1[i:i + 1, :], n_picked, cnt)
        picked = tail >= kth
        z = z + jnp.sum(jnp.where(picked, jnp.exp(tail - top), 0.0), axis=0, keepdims=True)
        tail_cnt = jnp.where(picked, 1.0, 0.0)
        for r in range(PEER_TOPK - few):
            cnt = jnp.where(s1 == v1[few + r:few + r + 1, :], tail_cnt[r:r + 1, :], cnt)
        e1_ref[h] = jnp.exp(s1 - v1[0:1, :]) / z
        cnt_ref[h] = cnt
        rank_ref[h] = rank2.astype(BF16)
        e2_ref[h] = jnp.exp(s2 - v2[0:1, :]).astype(BF16)


def _route(x, lnw, wq, k1, k2, tile):
    wq_pieces = _split3(wq)
    wq_spec = pl.BlockSpec(wq.shape, lambda i: (0, 0), pipeline_mode=pl.Buffered(1))
    n, d = x.shape
    per_head = pl.BlockSpec((N_HEADS, PEER_KEYS, tile), lambda i: (0, 0, i))
    head_shape = lambda dt: jax.ShapeDtypeStruct((N_HEADS, PEER_KEYS, n), dt)
    return pl.pallas_call(
        _route_kernel,
        grid=(n // tile,),
        in_specs=[pl.BlockSpec((tile, d), lambda i: (i, 0)),
                  pl.BlockSpec((1, d), lambda i: (0, 0)),
                  wq_spec, wq_spec, wq_spec,
                  pl.BlockSpec(k1.shape, lambda i: (0, 0)),
                  pl.BlockSpec(k2.shape, lambda i: (0, 0))],
        out_specs=[pl.BlockSpec((tile, d), lambda i: (i, 0)), per_head, per_head, per_head, per_head],
        out_shape=[jax.ShapeDtypeStruct((n, d), BF16), head_shape(F32), head_shape(F32),
                   head_shape(BF16), head_shape(BF16)],
        compiler_params=_params("parallel"),
        name="peer_route",
    )(x, lnw, *wq_pieces, k1, k2)


def _gelu_tanh(x):
    c = 0.7978845608028654
    return 0.5 * x * (1.0 + jnp.tanh(c * (x + 0.044715 * (x * x * x))))


def _peer_kernel(x_ref, h_ref, u_ref, vt_ref, e1_ref, cnt_ref, rank_ref, e2_ref, y_ref, acc_ref, w_ref,
                 *, expert_tile):
    j = pl.program_id(1)
    n_tiles = pl.num_programs(1) - 1
    tile = h_ref.shape[0]
    piece_rows = expert_tile // PEER_PIECES
    groups = piece_rows // PEER_KEYS

    @pl.when(j == 0)
    def _():
        acc_ref[...] = jnp.zeros_like(acc_ref)
        w_ref[1] = jnp.zeros((expert_tile, tile), BF16)

    cur = lax.rem(j, 2)
    prev = 1 - cur
    first_key = jnp.minimum(j, n_tiles - 1) * (expert_tile // PEER_KEYS)

    def activation(s):
        rows = slice(s * piece_rows, (s + 1) * piece_rows)
        return _gelu_tanh(_dot_nt(u_ref[rows, :], h_ref[...])).astype(BF16)

    zero = jnp.zeros((PEER_KEYS, tile), BF16)
    act_next = activation(0)
    drained = None
    for s in range(PEER_PIECES):
        act = act_next
        if s + 1 < PEER_PIECES:
            act_next = activation(s + 1)
        rows = slice(s * piece_rows, (s + 1) * piece_rows)
        part = _dot(vt_ref[:, rows], w_ref[prev, rows, :])
        drained = part if drained is None else drained + part
        for a_local in range(groups):
            a = first_key + s * groups + a_local
            gate = zero
            for h in range(N_HEADS):
                cnt_row = cnt_ref[h, pl.ds(a, 1), :].astype(BF16)
                e1_row = e1_ref[h, pl.ds(a, 1), :].astype(BF16)
                gate = gate + jnp.where(rank_ref[h] < cnt_row, e2_ref[h], zero) * e1_row
            local = slice(a_local * PEER_KEYS, (a_local + 1) * PEER_KEYS)
            out_rows = slice(s * piece_rows + a_local * PEER_KEYS, s * piece_rows + (a_local + 1) * PEER_KEYS)
            w_ref[cur, out_rows, :] = gate * act[local, :]
    acc_ref[...] += drained

    @pl.when(j == n_tiles)
    def _():
        y_ref[...] = x_ref[...] + acc_ref[...].T


def _peer(x, hn, u, vt, e1, cnt, rank2, e2, tile, expert_tile):
    n, d = x.shape
    n_tiles = u.shape[0] // expert_tile
    per_head = pl.BlockSpec((N_HEADS, PEER_KEYS, tile), lambda i, j: (0, 0, i))
    return pl.pallas_call(
        functools.partial(_peer_kernel, expert_tile=expert_tile),
        grid=(n // tile, n_tiles + 1),
        in_specs=[pl.BlockSpec((tile, d), lambda i, j: (i, 0)),
                  pl.BlockSpec((tile, d), lambda i, j: (i, 0)),
                  pl.BlockSpec((expert_tile, d), lambda i, j: (jnp.minimum(j, n_tiles - 1), 0)),
                  pl.BlockSpec((d, expert_tile), lambda i, j: (0, jnp.maximum(j - 1, 0))),
                  per_head, per_head, per_head, per_head],
        out_specs=pl.BlockSpec((tile, d), lambda i, j: (i, 0)),
        out_shape=jax.ShapeDtypeStruct((n, d), F32),
        scratch_shapes=[pltpu.VMEM((d, tile), F32), pltpu.VMEM((2, expert_tile, tile), BF16)],
        compiler_params=_params("parallel", "arbitrary"),
        name="peer_dense",
    )(x, hn, u, vt, e1, cnt, rank2, e2)


def _norm_kernel(x_ref, w_ref, y_ref):
    x = x_ref[...]
    y_ref[...] = x * lax.rsqrt(jnp.mean(x * x, axis=-1, keepdims=True) + RMS_EPS) * w_ref[...]


def _final_norm(x, w, tile):
    n, d = x.shape
    return pl.pallas_call(
        _norm_kernel,
        grid=(n // tile,),
        in_specs=[pl.BlockSpec((tile, d), lambda i: (i, 0)), pl.BlockSpec((1, d), lambda i: (0, 0))],
        out_specs=pl.BlockSpec((tile, d), lambda i: (i, 0)),
        out_shape=jax.ShapeDtypeStruct((n, d), F32),
        compiler_params=_params("parallel"),
        name="final_norm",
    )(x, w)


ROW_TILE = 256
PEER_TOKEN_TILE = 512
PEER_EXPERT_TILE = 1024
PEER_PIECES = 4
PAGES_PER_STEP = 8


def _heads_major(a):
    b, t, _ = a.shape
    return a.reshape(b, t, N_HEADS, HEAD_DIM).transpose(0, 2, 1, 3)


def _pad_lanes(a, width):
    return jnp.pad(a, (0, width - a.shape[0]))[None]


def kernel(x_prompt, x_sample, cache_sb_k, cache_sb_v, page_table, state_gdn, state_conv,
           meta_tokens, ln1_w, ln2_w, lnf_w, w_in, conv_w, gdn_a_log, gdn_dt_bias, gdn_norm_w,
           sb_logit_bias, w_out, peer_wq, peer_k1, peer_k2, peer_u, peer_v):
    bp, seq, d = x_prompt.shape
    bs, ts, _ = x_sample.shape
    depth = w_in.shape[0]
    pad = (-N_META) % SB_BLOCK
    tp = pad + N_META + seq
    np_rows = bp * tp
    ns_rows = bs * ts
    hist = CONV_TAPS - 1

    meta = jnp.broadcast_to(meta_tokens[None], (bp, N_META, d))
    xp = jnp.concatenate([jnp.zeros((bp, pad, d), F32), meta, x_prompt], axis=1)
    x = jnp.concatenate([xp.reshape(np_rows, d), x_sample.reshape(ns_rows, d)], axis=0)

    cache_kt = cache_sb_k.transpose(0, 1, 3, 4, 2)
    cache_vt = cache_sb_v.transpose(0, 1, 3, 4, 2)
    zero_state = jnp.zeros((bp, N_HEADS, HEAD_DIM, HEAD_DIM), F32)
    zero_conv = jnp.zeros((bp, hist, QKV_WIDTH), F32)

    outs = {k: [] for k in ("kp", "vp", "ks", "vs", "sp", "ss", "cp", "cs")}
    for l in range(depth):
        wl = w_in[l]
        o_gate = QKV_WIDTH
        o_a = o_gate + WIDTH
        o_qb = o_a + 2 * N_HEADS
        w_perm = jnp.concatenate(
            [wl[:, :o_a], wl[:, o_qb:], wl[:, o_a:o_qb],
             jnp.zeros((d, LANES - 2 * N_HEADS), F32)], axis=1).astype(BF16)
        qkv, gate, ab, qb, kb, vb, kb16, vb16 = _inproj(x, ln1_w[l][None], w_perm, ROW_TILE)

        gdn_consts = (conv_w[l], _pad_lanes(gdn_a_log[l], LANES), _pad_lanes(gdn_dt_bias[l], LANES),
                      jnp.tile(gdn_norm_w[l], QUAD)[None])
        split = lambda a, w_: (a[:np_rows].reshape(bp, tp, w_), a[np_rows:].reshape(bs, ts, w_))
        qkv_p, qkv_s = split(qkv, QKV_WIDTH)
        ab_p, ab_s = split(ab, LANES)
        gate_p, gate_s = split(gate, WIDTH)
        oa_p, s_p, c_p = _gdn(qkv_p, ab_p, gate_p, *gdn_consts, zero_state, zero_conv, GDN_CHUNK)
        oa_s, s_s, c_s = _gdn(qkv_s, ab_s, gate_s, *gdn_consts, state_gdn[l], state_conv[l], ts)

        qb_p, qb_s = split(qb, WIDTH)
        kb16_p, kb16_s = split(kb16, WIDTH)
        vb16_p, vb16_s = split(vb16, WIDTH)
        ob_p = _sb_prompt(qb_p, kb16_p, vb16_p, sb_logit_bias[l], pad).reshape(np_rows, WIDTH)

        bias_rows = jnp.broadcast_to(jnp.repeat(sb_logit_bias[l], ts)[:, None], (N_HEADS * ts, LANES))
        grow_t = lambda a: jnp.pad(_heads_major(a).transpose(0, 1, 3, 2), ((0, 0), (0, 0), (0, 0), (0, PAGE - ts)))
        ob_s = _sb_sample(page_table, _heads_major(qb_s), bias_rows, grow_t(kb16_s), grow_t(vb16_s),
                          cache_kt, cache_vt, l, PAGES_PER_STEP).reshape(ns_rows, WIDTH)

        oa = jnp.concatenate([oa_p.reshape(np_rows, WIDTH), oa_s.reshape(ns_rows, WIDTH)], axis=0)
        ob = jnp.concatenate([ob_p, ob_s], axis=0)
        x = _outproj(x, oa, ob, w_out[l].astype(BF16), ROW_TILE)

        hn, e1, cnt, rank2, e2 = _route(x, ln2_w[l][None], peer_wq[l], peer_k1[l], peer_k2[l], ROW_TILE)
        x = _peer(x, hn, peer_u[l].astype(BF16), peer_v[l].astype(BF16).T, e1, cnt, rank2, e2,
                  PEER_TOKEN_TILE, PEER_EXPERT_TILE)

        kb_p, kb_s = split(kb, WIDTH)
        vb_p, vb_s = split(vb, WIDTH)
        outs["kp"].append(kb_p[:, pad:].reshape(bp, tp - pad, N_HEADS, HEAD_DIM))
        outs["vp"].append(vb_p[:, pad:].reshape(bp, tp - pad, N_HEADS, HEAD_DIM))
        outs["ks"].append(kb_s.reshape(bs, ts, N_HEADS, HEAD_DIM))
        outs["vs"].append(vb_s.reshape(bs, ts, N_HEADS, HEAD_DIM))
        outs["sp"].append(s_p); outs["ss"].append(s_s)
        outs["cp"].append(c_p); outs["cs"].append(c_s)

    y = _final_norm(x, lnf_w[None], ROW_TILE)
    y_prompt = y[:np_rows].reshape(bp, tp, d)[:, pad + N_META:]
    y_sample = y[np_rows:].reshape(bs, ts, d)
    stack = lambda key: jnp.stack(outs[key])
    return (y_prompt, y_sample, stack("kp"), stack("vp"), stack("ks"), stack("vs"),
            stack("sp"), stack("ss"), stack("cp"), stack("cs"))
```

```python
import functools

import jax
import jax.numpy as jnp
from jax import lax
from jax.experimental import pallas as pl
from jax.experimental.pallas import tpu as pltpu

F32 = jnp.float32
BF16 = jnp.bfloat16
HIGHEST = lax.Precision.HIGHEST

HEAD_DIM = 64
N_HEADS = 8
WIDTH = N_HEADS * HEAD_DIM
QKV_WIDTH = 3 * WIDTH
QUAD = 4
QUAD_WIDTH = QUAD * HEAD_DIM
CONV_TAPS = 4
N_META = 16
GDN_CHUNK = 64
SB_BLOCK = 128
SB_GROUP = 4
PAGE = 128
PEER_KEYS = 128
PEER_TOPK = 16
RMS_EPS = 1e-6
L2_EPS = 1e-6
LOG2E = 1.4426950408889634
LANES = 128
VMEM_LIMIT = 56 * 1024 * 1024


def _dot(a, b, precision=None):
    return jnp.dot(a, b, preferred_element_type=F32, precision=precision)


def _dot_nt(a, b, precision=None):
    return lax.dot_general(a, b, (((1,), (1,)), ((), ())),
                           preferred_element_type=F32, precision=precision)


def _mm(a, b):
    return _dot(a.astype(BF16), b.astype(BF16))


def _mm_nt(a, b):
    return _dot_nt(a.astype(BF16), b.astype(BF16))


def _split3(x):
    hi = x.astype(BF16)
    rest = x - hi.astype(F32)
    mid = rest.astype(BF16)
    lo = (rest - mid.astype(F32)).astype(BF16)
    return hi, mid, lo


def _params(*sem):
    return pltpu.CompilerParams(dimension_semantics=sem, vmem_limit_bytes=VMEM_LIMIT)


def _iota(shape, axis):
    return lax.broadcasted_iota(jnp.int32, shape, axis)


def _log2_int(n):
    assert n & (n - 1) == 0, n
    return n.bit_length() - 1


def _inproj_kernel(x_ref, lnw_ref, w_ref, qkv_ref, gate_ref, ab_ref, qb_ref,
                   kb_ref, vb_ref, kb16_ref, vb16_ref):
    x = x_ref[...]
    h = x * lax.rsqrt(jnp.mean(x * x, axis=-1, keepdims=True) + RMS_EPS) * lnw_ref[...]
    h = h.astype(BF16)
    o = 0
    qkv_ref[...] = _dot(h, w_ref[:, o:o + QKV_WIDTH]); o += QKV_WIDTH
    gate_ref[...] = _dot(h, w_ref[:, o:o + WIDTH]); o += WIDTH
    qb_ref[...] = (_dot(h, w_ref[:, o:o + WIDTH]) * (LOG2E * HEAD_DIM ** -0.5)).astype(BF16); o += WIDTH
    kb = _dot(h, w_ref[:, o:o + WIDTH]); o += WIDTH
    kb_ref[...] = kb
    kb16_ref[...] = kb.astype(BF16)
    vb = _dot(h, w_ref[:, o:o + WIDTH]); o += WIDTH
    vb_ref[...] = vb
    vb16_ref[...] = vb.astype(BF16)
    ab_ref[...] = _dot(h, w_ref[:, o:o + LANES])


def _inproj(x, lnw, w, tile):
    n, d = x.shape
    wcols = w.shape[1]
    row = lambda width: pl.BlockSpec((tile, width), lambda i: (i, 0))
    shp = lambda width, dt: jax.ShapeDtypeStruct((n, width), dt)
    return pl.pallas_call(
        _inproj_kernel,
        grid=(n // tile,),
        in_specs=[row(d), pl.BlockSpec((1, d), lambda i: (0, 0)),
                  pl.BlockSpec((d, wcols), lambda i: (0, 0))],
        out_specs=[row(QKV_WIDTH), row(WIDTH), row(LANES), row(WIDTH), row(WIDTH),
                   row(WIDTH), row(WIDTH), row(WIDTH)],
        out_shape=[shp(QKV_WIDTH, F32), shp(WIDTH, F32), shp(LANES, F32), shp(WIDTH, BF16),
                   shp(WIDTH, F32), shp(WIDTH, F32), shp(WIDTH, BF16), shp(WIDTH, BF16)],
        compiler_params=_params("parallel"),
        name="inproj",
    )(x, lnw, w)


def _silu(x):
    return x * (1.0 / (1.0 + jnp.exp(-x)))


def _softplus(x):
    return jnp.maximum(x, 0.0) + jnp.log1p(jnp.exp(-jnp.abs(x)))


def _dot_sel(x, sel):
    hi, mid, lo = _split3(x)
    return _dot(hi, sel) + _dot(mid, sel) + _dot(lo, sel)


def _unit_lower_inverses(lows, rows, block):
    ri = _iota((rows, rows), 0)
    ci = _iota((rows, rows), 1)
    invs = [(ri == ci).astype(F32)] * len(lows)
    s = 1
    while s < block:
        off = ((ri & s) != 0) & ((ci & s) == 0) & ((ri ^ ci) < 2 * s)
        blks = [jnp.where(off, low, 0.0).astype(BF16) for low in lows]
        inv16 = [inv.astype(BF16) for inv in invs]
        right = [_dot(blk, i16) for blk, i16 in zip(blks, inv16)]
        invs = [inv - _dot(i16, r.astype(BF16)) for inv, i16, r in zip(invs, inv16, right)]
        s *= 2
    return invs


def _gdn_kernel(qkv_ref, ab_ref, gate_ref, convw_ref, alog_ref, dtb_ref, normw_ref,
                s0_ref, conv0_ref, o_ref, sout_ref, convout_ref, xbuf, s_scr, *, chunk):
    c = pl.program_id(1)
    hist = CONV_TAPS - 1
    rows = QUAD * chunk
    chunk_shift = _log2_int(chunk)
    head_shift = _log2_int(HEAD_DIM)
    n_quads = N_HEADS // QUAD

    @pl.when(c == 0)
    def _():
        xbuf[0:8, :] = jnp.zeros((8, QKV_WIDTH), F32)
        xbuf[8 - hist:8, :] = conv0_ref[0]
        s_scr[...] = jnp.zeros_like(s_scr)
        for h in range(N_HEADS):
            blk = slice((h % QUAD) * HEAD_DIM, (h % QUAD + 1) * HEAD_DIM)
            s_scr[h // QUAD, blk, blk] = s0_ref[0, h]

    xbuf[8:8 + chunk, :] = qkv_ref[0]
    conv = convw_ref[0:1, :] * xbuf[8 - hist:8 - hist + chunk, :]
    for i in range(1, CONV_TAPS):
        conv = conv + convw_ref[i:i + 1, :] * xbuf[8 - hist + i:8 - hist + i + chunk, :]
    convout_ref[0] = xbuf[8 + chunk - hist:8 + chunk, :]
    tail = xbuf[chunk:chunk + 8, :]
    xbuf[0:8, :] = tail
    act = _silu(conv)

    ab = ab_ref[0]
    lane = _iota((chunk, LANES), 1)
    g = -jnp.exp(alog_ref[...]) * _softplus(ab + dtb_ref[...])
    beta = 1.0 / (1.0 + jnp.exp(-ab))
    tril = (_iota((chunk, chunk), 0) >= _iota((chunk, chunk), 1)).astype(BF16)
    g_hi, g_mid, g_lo = _split3(g)
    gcum = _dot(tril, g_hi) + _dot(tril, g_mid) + _dot(tril, g_lo)
    eg_beta = jnp.where(lane < N_HEADS, jnp.exp(gcum), beta)
    edec = jnp.exp(gcum[chunk - 1:chunk, :] - gcum)

    lane_head = lax.shift_right_logical(_iota((LANES, QUAD_WIDTH), 1), head_shift)
    src_lane = _iota((LANES, QUAD_WIDTH), 0)
    block_ones = (lax.shift_right_logical(_iota((QUAD_WIDTH, QUAD_WIDTH), 0), head_shift)
                  == lax.shift_right_logical(_iota((QUAD_WIDTH, QUAD_WIDTH), 1), head_shift)).astype(BF16)
    ones_k = jnp.ones((LANES, rows), BF16)
    ri = _iota((rows, rows), 0)
    ci = _iota((rows, rows), 1)
    same = lax.shift_right_logical(ri, chunk_shift) == lax.shift_right_logical(ci, chunk_shift)
    incl = same & (ri >= ci)
    strict = same & (ri > ci)
    exp_row_head = lax.shift_right_logical(_iota((rows, QUAD_WIDTH), 0), chunk_shift)
    exp_lane_head = lax.shift_right_logical(_iota((rows, QUAD_WIDTH), 1), head_shift)
    own_block = exp_row_head == exp_lane_head
    stack_row_head = lax.shift_right_logical(_iota((rows, LANES), 0), chunk_shift)
    stack_lane = _iota((rows, LANES), 1)

    def expand(x):
        return jnp.where(own_block, jnp.concatenate([x] * QUAD, axis=0), 0.0)

    quads = range(n_quads)
    ones_r = jnp.ones((rows, LANES), BF16)

    def prepare(quad):
        first = quad * QUAD
        sel_g = (src_lane == lane_head + first).astype(BF16)
        sel_b = (src_lane == lane_head + (N_HEADS + first)).astype(BF16)
        eg_w = _dot_sel(eg_beta, sel_g)
        beta_w = _dot_sel(eg_beta, sel_b)
        edec_w = _dot_sel(edec, sel_g)
        q = act[:, quad * QUAD_WIDTH:(quad + 1) * QUAD_WIDTH]
        k = act[:, WIDTH + quad * QUAD_WIDTH:WIDTH + (quad + 1) * QUAD_WIDTH]
        v = act[:, 2 * WIDTH + quad * QUAD_WIDTH:2 * WIDTH + (quad + 1) * QUAD_WIDTH]
        q = q * lax.rsqrt(_dot_sel(q * q, block_ones) + L2_EPS) * (HEAD_DIM ** -0.5)
        k = k * lax.rsqrt(_dot_sel(k * k, block_ones) + L2_EPS)
        kbeta = k * beta_w
        stacked = jnp.where(stack_lane == stack_row_head + first,
                            jnp.concatenate([gcum] * QUAD, axis=0), 0.0)
        s_hi, s_mid, s_lo = _split3(stacked)
        g_row = _dot(s_hi, ones_k) + _dot(s_mid, ones_k) + _dot(s_lo, ones_k)
        g_col = _dot_nt(ones_r, s_hi) + _dot_nt(ones_r, s_mid) + _dot_nt(ones_r, s_lo)
        decay = jnp.where(incl, jnp.exp(g_row - g_col), 0.0)
        e_k = expand(k).astype(BF16)
        low = jnp.where(strict, _dot_nt(expand(kbeta).astype(BF16), e_k) * decay, 0.0)
        return dict(decay=decay, e_k=e_k, low=low, eg_last=eg_w[chunk - 1:chunk, :],
                    e_q=expand(q).astype(BF16), e_qg=expand(q * eg_w).astype(BF16),
                    e_vb=expand(v * beta_w).astype(BF16), e_kg=expand(kbeta * eg_w).astype(BF16),
                    e_kd=expand(k * edec_w))

    pre = [prepare(quad) for quad in quads]
    invs = _unit_lower_inverses([p["low"] for p in pre], rows, chunk)
    invs = [inv.astype(BF16) for inv in invs]
    states = [s_scr[quad] for quad in quads]
    states16 = [st.astype(BF16) for st in states]
    sol_v = [_dot(invs[i], pre[i]["e_vb"]) for i in quads]
    sol_k = [_dot(invs[i], pre[i]["e_kg"]) for i in quads]
    attn = [_dot_nt(pre[i]["e_q"], pre[i]["e_k"]) * pre[i]["decay"] for i in quads]
    from_state = [_dot(pre[i]["e_qg"], states16[i]) for i in quads]
    u = [sol_v[i] - _mm(sol_k[i], states16[i]) for i in quads]
    e_o = [from_state[i] + _mm(attn[i], u[i]) for i in quads]
    for i in quads:
        s_scr[i] = states[i] * pre[i]["eg_last"] + lax.dot_general(
            pre[i]["e_kd"], u[i], (((0,), (0,)), ((), ())), preferred_element_type=F32)
    for i in quads:
        o = e_o[i][0:chunk, :]
        for h in range(1, QUAD):
            o = o + e_o[i][h * chunk:(h + 1) * chunk, :]
        on = o * lax.rsqrt(_dot_sel(o * o, block_ones) * (1.0 / HEAD_DIM) + RMS_EPS)
        cols = slice(i * QUAD_WIDTH, (i + 1) * QUAD_WIDTH)
        o_ref[0, :, cols] = (on * normw_ref[...] * _silu(gate_ref[0, :, cols])).astype(o_ref.dtype)

    @pl.when(c == pl.num_programs(1) - 1)
    def _():
        for h in range(N_HEADS):
            blk = slice((h % QUAD) * HEAD_DIM, (h % QUAD + 1) * HEAD_DIM)
            sout_ref[0, h] = s_scr[h // QUAD, blk, blk]


def _gdn(qkv, ab, gate, convw, alog, dtb, normw, s0, conv0, chunk):
    b, t, _ = qkv.shape
    n = t // chunk
    tok = lambda width: pl.BlockSpec((1, chunk, width), lambda i, c: (i, c, 0))
    const2 = lambda r, cdim: pl.BlockSpec((r, cdim), lambda i, c: (0, 0))
    hist = CONV_TAPS - 1
    state_spec = pl.BlockSpec((1, N_HEADS, HEAD_DIM, HEAD_DIM), lambda i, c: (i, 0, 0, 0))
    conv_spec = pl.BlockSpec((1, hist, QKV_WIDTH), lambda i, c: (i, 0, 0))
    return pl.pallas_call(
        functools.partial(_gdn_kernel, chunk=chunk),
        grid=(b, n),
        in_specs=[tok(QKV_WIDTH), tok(LANES), tok(WIDTH),
                  const2(CONV_TAPS, QKV_WIDTH), const2(1, LANES), const2(1, LANES),
                  const2(1, QUAD_WIDTH), state_spec, conv_spec],
        out_specs=[tok(WIDTH), state_spec, conv_spec],
        out_shape=[jax.ShapeDtypeStruct((b, t, WIDTH), BF16),
                   jax.ShapeDtypeStruct((b, N_HEADS, HEAD_DIM, HEAD_DIM), F32),
                   jax.ShapeDtypeStruct((b, hist, QKV_WIDTH), F32)],
        scratch_shapes=[pltpu.VMEM((chunk + 8, QKV_WIDTH), F32),
                        pltpu.VMEM((N_HEADS // QUAD, QUAD_WIDTH, QUAD_WIDTH), F32)],
        compiler_params=_params("parallel", "arbitrary"),
        name=f"gdn_c{chunk}",
    )(qkv, ab, gate, convw, alog, dtb, normw, s0, conv0)


def _suffix_total():
    r = _iota((SB_BLOCK, 2 * SB_BLOCK), 0)
    c = _iota((SB_BLOCK, 2 * SB_BLOCK), 1)
    return ((r > c) | (c >= SB_BLOCK)).astype(BF16)


def _sb_log_parts(z, mask):
    sp = jnp.log2(1.0 + jnp.exp2(-jnp.abs(z)))
    log_beta = jnp.minimum(z, 0.0) - sp
    log_stay = log_beta - z
    if mask is not None:
        log_stay = jnp.where(mask, log_stay, 0.0)
    return log_beta, log_stay.astype(BF16)


def _sb_logs(z, mask, suffix_total):
    log_beta, log_stay = _sb_log_parts(z, mask)
    return log_beta, _dot(log_stay, suffix_total)


def _sb_weights(logs, mask, run):
    log_beta, sums = logs
    w = jnp.exp2(log_beta + run + sums[:, :SB_BLOCK])
    if mask is not None:
        w = jnp.where(mask, w, 0.0)
    return w, run + sums[:, SB_BLOCK:]


def _sb_block(z, mask, run, suffix_total):
    return _sb_weights(_sb_logs(z, mask, suffix_total), mask, run)


def _sb_prompt_kernel(bias_ref, q_ref, k_ref, v_ref, o_ref, qm_ref, acc_ref, run_ref, *, pad):
    i = pl.program_id(1)
    pair_lanes = 2 * HEAD_DIM
    n_pairs = N_HEADS // 2
    ri = _iota((SB_BLOCK, SB_BLOCK), 0)
    ci = _iota((SB_BLOCK, SB_BLOCK), 1)
    suffix_total = _suffix_total()

    for p in range(n_pairs):
        qp = q_ref[0, :, p * pair_lanes:(p + 1) * pair_lanes]
        qm_ref[2 * p] = jnp.where(ci < HEAD_DIM, qp, jnp.zeros_like(qp))
        qm_ref[2 * p + 1] = jnp.where(ci >= HEAD_DIM, qp, jnp.zeros_like(qp))
    acc_ref[...] = jnp.zeros_like(acc_ref)
    run_ref[...] = jnp.zeros_like(run_ref)

    def visit(blocks):
        kv = []
        for j, _ in blocks:
            start = pl.multiple_of(j * SB_BLOCK, SB_BLOCK)
            kv.append([(k_ref[0, pl.ds(start, SB_BLOCK), p * pair_lanes:(p + 1) * pair_lanes],
                        v_ref[0, pl.ds(start, SB_BLOCK), p * pair_lanes:(p + 1) * pair_lanes])
                       for p in range(n_pairs)])
        heads = range(N_HEADS)
        nb = range(len(blocks))
        runs = [run_ref[h] for h in heads]
        zs = [[_dot_nt(qm_ref[h], kv[b][h // 2][0]) + bias_ref[h] * LOG2E for h in heads] for b in nb]
        logs = [[_sb_log_parts(zs[b][h], blocks[b][1]) for h in heads] for b in nb]
        sums = [[_dot(logs[b][h][1], suffix_total) for h in heads] for b in nb]
        ws = []
        for b in nb:
            ws.append([])
            for h in heads:
                w, runs[h] = _sb_weights((logs[b][h][0], sums[b][h]), blocks[b][1], runs[h])
                ws[b].append(w.astype(BF16))
        parts = [[_dot(ws[b][h], kv[b][h // 2][1]) for h in heads] for b in nb]
        for h in heads:
            run_ref[h] = runs[h]
            out = parts[0][h]
            for b in nb[1:]:
                out = out + parts[b][h]
            acc_ref[h] += out

    visit([(i, (ci < ri) & (ci + i * SB_BLOCK >= pad))])

    n_mid = jnp.maximum(i - 1, 0)
    rem = lax.rem(n_mid, SB_GROUP)

    def single(step, carry):
        visit([(i - 1 - step, None)])
        return carry

    lax.fori_loop(0, rem, single, 0)

    def group(step, carry):
        j = i - 1 - rem - SB_GROUP * step
        visit([(j - b, None) for b in range(SB_GROUP)])
        return carry

    lax.fori_loop(0, n_mid // SB_GROUP, group, 0)

    @pl.when(i > 0)
    def _():
        visit([(0, ci >= pad)])

    for p in range(n_pairs):
        o_ref[0, :, p * pair_lanes:(p + 1) * pair_lanes] = jnp.where(
            ci < HEAD_DIM, acc_ref[2 * p], acc_ref[2 * p + 1]).astype(o_ref.dtype)


def _sb_prompt(q, k, v, bias, pad):
    b, t, width = q.shape
    nq = t // SB_BLOCK
    resident = pl.BlockSpec((1, t, width), lambda bi, i: (bi, 0, 0), pipeline_mode=pl.Buffered(1))
    return pl.pallas_call(
        functools.partial(_sb_prompt_kernel, pad=pad),
        grid=(b, nq),
        in_specs=[pl.BlockSpec(memory_space=pltpu.SMEM),
                  pl.BlockSpec((1, SB_BLOCK, width), lambda bi, i: (bi, i, 0)),
                  resident, resident],
        out_specs=pl.BlockSpec((1, SB_BLOCK, width), lambda bi, i: (bi, i, 0)),
        out_shape=jax.ShapeDtypeStruct((b, t, width), BF16),
        scratch_shapes=[pltpu.VMEM((N_HEADS, SB_BLOCK, SB_BLOCK), BF16),
                        pltpu.VMEM((N_HEADS, SB_BLOCK, SB_BLOCK), F32),
                        pltpu.VMEM((N_HEADS, SB_BLOCK, SB_BLOCK), F32)],
        compiler_params=_params("parallel", "arbitrary"),
        name="sb_prompt",
    )(bias, q, k, v)


def _sb_sample_kernel(pt_ref, q_ref, bias_ref, knew_ref, vnew_ref, *refs, pages_per_step, t_new):
    k_refs = refs[:pages_per_step]
    v_refs = refs[pages_per_step:2 * pages_per_step]
    o_ref = refs[2 * pages_per_step]
    acc_ref, run_ref = refs[2 * pages_per_step + 1:]
    g = pl.program_id(1)
    rows = N_HEADS * t_new
    heads = range(N_HEADS)
    bias = bias_ref[...] * LOG2E
    suffix_total = _suffix_total()

    def logits(kt_of_head):
        return jnp.concatenate([_dot(q_ref[0, h], kt_of_head(h)) for h in heads], axis=0) + bias

    def outputs(w, vt_of_head):
        return [_dot_nt(w[h * t_new:(h + 1) * t_new, :].astype(BF16), vt_of_head(h)) for h in heads]

    def store_state(run, accs):
        run_ref[...] = run
        for h in heads:
            acc_ref[h] = accs[h]

    @pl.when(g == 0)
    def _():
        ri = _iota((rows, LANES), 0)
        ci = _iota((rows, LANES), 1)
        mask = ci < lax.rem(ri, t_new)
        z = logits(lambda h: knew_ref[0, h])
        w, run = _sb_block(z, mask, jnp.zeros((rows, LANES), F32), suffix_total)
        store_state(run, outputs(w, lambda h: vnew_ref[0, h]))

    def page_head(ref):
        return lambda h: ref[0, 0, h].astype(BF16)

    order = list(reversed(range(pages_per_step)))
    zs = [logits(page_head(k_refs[p])) for p in order]
    logs = [_sb_log_parts(z, None) for z in zs]
    sums = [_dot(log_stay, suffix_total) for _, log_stay in logs]
    run = run_ref[...]
    ws = []
    for (log_beta, _), s in zip(logs, sums):
        w, run = _sb_weights((log_beta, s), None, run)
        ws.append(w)
    accs = [acc_ref[h] for h in heads]
    for w, p in zip(ws, order):
        outs = outputs(w, page_head(v_refs[p]))
        accs = [accs[h] + outs[h] for h in heads]
    store_state(run, accs)

    @pl.when(g == pl.num_programs(1) - 1)
    def _():
        for h in heads:
            o_ref[0, :, h * HEAD_DIM:(h + 1) * HEAD_DIM] = acc_ref[h].astype(o_ref.dtype)


def _sb_sample(page_table, q, bias_rows, knew_t, vnew_t, cache_kt, cache_vt, layer, pages_per_step):
    b, _, t_new, _ = q.shape
    rows = N_HEADS * t_new
    n_pages = page_table.shape[1]
    n_groups = n_pages // pages_per_step

    def page_spec(p):
        def index(bi, g, pt):
            return (layer, pt[bi, (n_groups - 1 - g) * pages_per_step + p], 0, 0, 0)
        return pl.BlockSpec((1, 1, N_HEADS, HEAD_DIM, PAGE), index)

    per_batch = lambda r, c: pl.BlockSpec((1, N_HEADS, r, c), lambda bi, g, pt: (bi, 0, 0, 0))
    return pl.pallas_call(
        functools.partial(_sb_sample_kernel, pages_per_step=pages_per_step, t_new=t_new),
        grid_spec=pltpu.PrefetchScalarGridSpec(
            num_scalar_prefetch=1,
            grid=(b, n_groups),
            in_specs=[per_batch(t_new, HEAD_DIM),
                      pl.BlockSpec((rows, LANES), lambda bi, g, pt: (0, 0)),
                      per_batch(HEAD_DIM, PAGE), per_batch(HEAD_DIM, PAGE)]
                     + [page_spec(p) for p in range(pages_per_step)] * 2,
            out_specs=pl.BlockSpec((1, t_new, WIDTH), lambda bi, g, pt: (bi, 0, 0)),
            scratch_shapes=[pltpu.VMEM((N_HEADS, t_new, HEAD_DIM), F32),
                            pltpu.VMEM((rows, LANES), F32)],
        ),
        out_shape=jax.ShapeDtypeStruct((b, t_new, WIDTH), BF16),
        compiler_params=_params("parallel", "arbitrary"),
        name="sb_sample",
    )(page_table, q, bias_rows, knew_t, vnew_t,
      *([cache_kt] * pages_per_step), *([cache_vt] * pages_per_step))


def _outproj_kernel(x_ref, oa_ref, ob_ref, w_ref, y_ref):
    y_ref[...] = (x_ref[...] + _dot(oa_ref[...], w_ref[0:WIDTH, :])
                  + _dot(ob_ref[...], w_ref[WIDTH:2 * WIDTH, :]))


def _outproj(x, oa, ob, w, tile):
    n, d = x.shape
    return pl.pallas_call(
        _outproj_kernel,
        grid=(n // tile,),
        in_specs=[pl.BlockSpec((tile, d), lambda i: (i, 0)),
                  pl.BlockSpec((tile, WIDTH), lambda i: (i, 0)),
                  pl.BlockSpec((tile, WIDTH), lambda i: (i, 0)),
                  pl.BlockSpec((2 * WIDTH, d), lambda i: (0, 0))],
        out_specs=pl.BlockSpec((tile, d), lambda i: (i, 0)),
        out_shape=jax.ShapeDtypeStruct((n, d), F32),
        compiler_params=_params("parallel"),
        name="outproj",
    )(x, oa, ob, w)


def _top_values_many(xs, count):
    vals = [[] for _ in xs]
    ranks = [jnp.full_like(x, float(count)) for x in xs]
    xs = list(xs)
    for r in range(count):
        ms = [jnp.max(x, axis=0, keepdims=True) for x in xs]
        hits = [x == m for x, m in zip(xs, ms)]
        ranks = [jnp.where(hit, float(r), rank) for hit, rank in zip(hits, ranks)]
        xs = [jnp.where(hit, -jnp.inf, x) for hit, x in zip(hits, xs)]
        for v, m in zip(vals, ms):
            v.append(m)
    return [jnp.concatenate(v, axis=0) for v in vals], ranks


def _top_values(x, count):
    return _top_values_many([x], count)[0][0]


def _route_kernel(x_ref, lnw_ref, wq_hi_ref, wq_mid_ref, wq_lo_ref, k1_ref, k2_ref,
                  h_ref, e1_ref, cnt_ref, rank_ref, e2_ref):
    x = x_ref[...]
    hn = x * lax.rsqrt(jnp.mean(x * x, axis=-1, keepdims=True) + RMS_EPS) * lnw_ref[...]
    h_hi, h_mid, h_lo = _split3(hn)
    h_ref[...] = h_hi
    half = PEER_KEYS

    def query(cols):
        w_hi, w_mid, w_lo = wq_hi_ref[:, cols], wq_mid_ref[:, cols], wq_lo_ref[:, cols]
        return (_dot(h_lo, w_hi) + _dot(h_hi, w_lo) + _dot(h_mid, w_mid)
                + _dot(h_mid, w_hi) + _dot(h_hi, w_mid) + _dot(h_hi, w_hi))

    few = PEER_TOPK // 2

    head_cols = lambda h: slice(2 * h * half, (2 * h + 2) * half)
    q_next = query(head_cols(0))
    for h in range(N_HEADS):
        q = q_next
        if h + 1 < N_HEADS:
            q_next = query(head_cols(h + 1))
        q1, q2 = q[:, :half], q[:, half:]
        s1 = _dot_nt(k1_ref[...], q1, HIGHEST)
        s2 = _dot_nt(k2_ref[...], q2, HIGHEST)
        (v1, v2), (_, rank2) = _top_values_many([s1, s2], PEER_TOPK)
        top = v1[0:1, :] + v2[0:1, :]
        rows = [v1[0:1, :] + v2] + [v1[i:i + 1, :] + v2[0:few, :] for i in range(1, few)]
        tail = v1[few:, :] + v2[0:1, :]
        kth = _top_values(jnp.concatenate(rows + [tail], axis=0),
                          PEER_TOPK)[PEER_TOPK - 1:PEER_TOPK, :]
        z = jnp.zeros_like(top)
        cnt = jnp.zeros_like(s1)
        for i, sums in enumerate(rows):
            picked = sums >= kth
            z = z + jnp.sum(jnp.where(picked, jnp.exp(sums - top), 0.0), axis=0, keepdims=True)
            n_picked = jnp.sum(jnp.where(picked, 1.0, 0.0), axis=0, keepdims=True)
            cnt = jnp.where(s1 == v1[i:i + 1, :], n_picked, cnt)
        picked = tail >= kth
        z = z + jnp.sum(jnp.where(picked, jnp.exp(tail - top), 0.0), axis=0, keepdims=True)
        tail_cnt = jnp.where(picked, 1.0, 0.0)
        for r in range(PEER_TOPK - few):
            cnt = jnp.where(s1 == v1[few + r:few + r + 1, :], tail_cnt[r:r + 1, :], cnt)
        e1_ref[h] = jnp.exp(s1 - v1[0:1, :]) / z
        cnt_ref[h] = cnt
        rank_ref[h] = rank2.astype(BF16)
        e2_ref[h] = jnp.exp(s2 - v2[0:1, :]).astype(BF16)


def _route(x, lnw, wq, k1, k2, tile):
    wq_pieces = _split3(wq)
    wq_spec = pl.BlockSpec(wq.shape, lambda i: (0, 0), pipeline_mode=pl.Buffered(1))
    n, d = x.shape
    per_head = pl.BlockSpec((N_HEADS, PEER_KEYS, tile), lambda i: (0, 0, i))
    head_shape = lambda dt: jax.ShapeDtypeStruct((N_HEADS, PEER_KEYS, n), dt)
    return pl.pallas_call(
        _route_kernel,
        grid=(n // tile,),
        in_specs=[pl.BlockSpec((tile, d), lambda i: (i, 0)),
                  pl.BlockSpec((1, d), lambda i: (0, 0)),
                  wq_spec, wq_spec, wq_spec,
                  pl.BlockSpec(k1.shape, lambda i: (0, 0)),
                  pl.BlockSpec(k2.shape, lambda i: (0, 0))],
        out_specs=[pl.BlockSpec((tile, d), lambda i: (i, 0)), per_head, per_head, per_head, per_head],
        out_shape=[jax.ShapeDtypeStruct((n, d), BF16), head_shape(F32), head_shape(F32),
                   head_shape(BF16), head_shape(BF16)],
        compiler_params=_params("parallel"),
        name="peer_route",
    )(x, lnw, *wq_pieces, k1, k2)


def _gelu_tanh(x):
    c = 0.7978845608028654
    return 0.5 * x * (1.0 + jnp.tanh(c * (x + 0.044715 * (x * x * x))))


def _peer_kernel(x_ref, h_ref, u_ref, vt_ref, e1_ref, cnt_ref, rank_ref, e2_ref, y_ref, acc_ref, w_ref,
                 *, expert_tile):
    j = pl.program_id(1)
    n_tiles = pl.num_programs(1) - 1
    tile = h_ref.shape[0]
    piece_rows = expert_tile // PEER_PIECES
    groups = piece_rows // PEER_KEYS

    @pl.when(j == 0)
    def _():
        acc_ref[...] = jnp.zeros_like(acc_ref)
        w_ref[1] = jnp.zeros((expert_tile, tile), BF16)

    cur = lax.rem(j, 2)
    prev = 1 - cur
    first_key = jnp.minimum(j, n_tiles - 1) * (expert_tile // PEER_KEYS)

    def activation(s):
        rows = slice(s * piece_rows, (s + 1) * piece_rows)
        return _gelu_tanh(_dot_nt(u_ref[rows, :], h_ref[...])).astype(BF16)

    zero = jnp.zeros((PEER_KEYS, tile), BF16)
    act_next = activation(0)
    drained = None
    for s in range(PEER_PIECES):
        act = act_next
        if s + 1 < PEER_PIECES:
            act_next = activation(s + 1)
        rows = slice(s * piece_rows, (s + 1) * piece_rows)
        part = _dot(vt_ref[:, rows], w_ref[prev, rows, :])
        drained = part if drained is None else drained + part
        for a_local in range(groups):
            a = first_key + s * groups + a_local
            gate = zero
            for h in range(N_HEADS):
                cnt_row = cnt_ref[h, pl.ds(a, 1), :].astype(BF16)
                e1_row = e1_ref[h, pl.ds(a, 1), :].astype(BF16)
                gate = gate + jnp.where(rank_ref[h] < cnt_row, e2_ref[h], zero) * e1_row
            local = slice(a_local * PEER_KEYS, (a_local + 1) * PEER_KEYS)
            out_rows = slice(s * piece_rows + a_local * PEER_KEYS, s * piece_rows + (a_local + 1) * PEER_KEYS)
            w_ref[cur, out_rows, :] = gate * act[local, :]
    acc_ref[...] += drained

    @pl.when(j == n_tiles)
    def _():
        y_ref[...] = x_ref[...] + acc_ref[...].T


def _peer(x, hn, u, vt, e1, cnt, rank2, e2, tile, expert_tile):
    n, d = x.shape
    n_tiles = u.shape[0] // expert_tile
    per_head = pl.BlockSpec((N_HEADS, PEER_KEYS, tile), lambda i, j: (0, 0, i))
    return pl.pallas_call(
        functools.partial(_peer_kernel, expert_tile=expert_tile),
        grid=(n // tile, n_tiles + 1),
        in_specs=[pl.BlockSpec((tile, d), lambda i, j: (i, 0)),
                  pl.BlockSpec((tile, d), lambda i, j: (i, 0)),
                  pl.BlockSpec((expert_tile, d), lambda i, j: (jnp.minimum(j, n_tiles - 1), 0)),
                  pl.BlockSpec((d, expert_tile), lambda i, j: (0, jnp.maximum(j - 1, 0))),
                  per_head, per_head, per_head, per_head],
        out_specs=pl.BlockSpec((tile, d), lambda i, j: (i, 0)),
        out_shape=jax.ShapeDtypeStruct((n, d), F32),
        scratch_shapes=[pltpu.VMEM((d, tile), F32), pltpu.VMEM((2, expert_tile, tile), BF16)],
        compiler_params=_params("parallel", "arbitrary"),
        name="peer_dense",
    )(x, hn, u, vt, e1, cnt, rank2, e2)


def _norm_kernel(x_ref, w_ref, y_ref):
    x = x_ref[...]
    y_ref[...] = x * lax.rsqrt(jnp.mean(x * x, axis=-1, keepdims=True) + RMS_EPS) * w_ref[...]


def _final_norm(x, w, tile):
    n, d = x.shape
    return pl.pallas_call(
        _norm_kernel,
        grid=(n // tile,),
        in_specs=[pl.BlockSpec((tile, d), lambda i: (i, 0)), pl.BlockSpec((1, d), lambda i: (0, 0))],
        out_specs=pl.BlockSpec((tile, d), lambda i: (i, 0)),
        out_shape=jax.ShapeDtypeStruct((n, d), F32),
        compiler_params=_params("parallel"),
        name="final_norm",
    )(x, w)


ROW_TILE = 256
PEER_TOKEN_TILE = 512
PEER_EXPERT_TILE = 1024
PEER_PIECES = 4
PAGES_PER_STEP = 16


def _heads_major(a):
    b, t, _ = a.shape
    return a.reshape(b, t, N_HEADS, HEAD_DIM).transpose(0, 2, 1, 3)


def _pad_lanes(a, width):
    return jnp.pad(a, (0, width - a.shape[0]))[None]


def kernel(x_prompt, x_sample, cache_sb_k, cache_sb_v, page_table, state_gdn, state_conv,
           meta_tokens, ln1_w, ln2_w, lnf_w, w_in, conv_w, gdn_a_log, gdn_dt_bias, gdn_norm_w,
           sb_logit_bias, w_out, peer_wq, peer_k1, peer_k2, peer_u, peer_v):
    bp, seq, d = x_prompt.shape
    bs, ts, _ = x_sample.shape
    depth = w_in.shape[0]
    pad = (-N_META) % SB_BLOCK
    tp = pad + N_META + seq
    np_rows = bp * tp
    ns_rows = bs * ts
    hist = CONV_TAPS - 1

    meta = jnp.broadcast_to(meta_tokens[None], (bp, N_META, d))
    xp = jnp.concatenate([jnp.zeros((bp, pad, d), F32), meta, x_prompt], axis=1)
    x = jnp.concatenate([xp.reshape(np_rows, d), x_sample.reshape(ns_rows, d)], axis=0)

    cache_kt = cache_sb_k.transpose(0, 1, 3, 4, 2)
    cache_vt = cache_sb_v.transpose(0, 1, 3, 4, 2)
    zero_state = jnp.zeros((bp, N_HEADS, HEAD_DIM, HEAD_DIM), F32)
    zero_conv = jnp.zeros((bp, hist, QKV_WIDTH), F32)

    outs = {k: [] for k in ("kp", "vp", "ks", "vs", "sp", "ss", "cp", "cs")}
    for l in range(depth):
        wl = w_in[l]
        o_gate = QKV_WIDTH
        o_a = o_gate + WIDTH
        o_qb = o_a + 2 * N_HEADS
        w_perm = jnp.concatenate(
            [wl[:, :o_a], wl[:, o_qb:], wl[:, o_a:o_qb],
             jnp.zeros((d, LANES - 2 * N_HEADS), F32)], axis=1).astype(BF16)
        qkv, gate, ab, qb, kb, vb, kb16, vb16 = _inproj(x, ln1_w[l][None], w_perm, ROW_TILE)

        gdn_consts = (conv_w[l], _pad_lanes(gdn_a_log[l], LANES), _pad_lanes(gdn_dt_bias[l], LANES),
                      jnp.tile(gdn_norm_w[l], QUAD)[None])
        split = lambda a, w_: (a[:np_rows].reshape(bp, tp, w_), a[np_rows:].reshape(bs, ts, w_))
        qkv_p, qkv_s = split(qkv, QKV_WIDTH)
        ab_p, ab_s = split(ab, LANES)
        gate_p, gate_s = split(gate, WIDTH)
        oa_p, s_p, c_p = _gdn(qkv_p, ab_p, gate_p, *gdn_consts, zero_state, zero_conv, GDN_CHUNK)
        oa_s, s_s, c_s = _gdn(qkv_s, ab_s, gate_s, *gdn_consts, state_gdn[l], state_conv[l], ts)

        qb_p, qb_s = split(qb, WIDTH)
        kb16_p, kb16_s = split(kb16, WIDTH)
        vb16_p, vb16_s = split(vb16, WIDTH)
        ob_p = _sb_prompt(qb_p, kb16_p, vb16_p, sb_logit_bias[l], pad).reshape(np_rows, WIDTH)

        bias_rows = jnp.broadcast_to(jnp.repeat(sb_logit_bias[l], ts)[:, None], (N_HEADS * ts, LANES))
        grow_t = lambda a: jnp.pad(_heads_major(a).transpose(0, 1, 3, 2), ((0, 0), (0, 0), (0, 0), (0, PAGE - ts)))
        ob_s = _sb_sample(page_table, _heads_major(qb_s), bias_rows, grow_t(kb16_s), grow_t(vb16_s),
                          cache_kt, cache_vt, l, PAGES_PER_STEP).reshape(ns_rows, WIDTH)

        oa = jnp.concatenate([oa_p.reshape(np_rows, WIDTH), oa_s.reshape(ns_rows, WIDTH)], axis=0)
        ob = jnp.concatenate([ob_p, ob_s], axis=0)
        x = _outproj(x, oa, ob, w_out[l].astype(BF16), ROW_TILE)

        hn, e1, cnt, rank2, e2 = _route(x, ln2_w[l][None], peer_wq[l], peer_k1[l], peer_k2[l], ROW_TILE)
        x = _peer(x, hn, peer_u[l].astype(BF16), peer_v[l].astype(BF16).T, e1, cnt, rank2, e2,
                  PEER_TOKEN_TILE, PEER_EXPERT_TILE)

        kb_p, kb_s = split(kb, WIDTH)
        vb_p, vb_s = split(vb, WIDTH)
        outs["kp"].append(kb_p[:, pad:].reshape(bp, tp - pad, N_HEADS, HEAD_DIM))
        outs["vp"].append(vb_p[:, pad:].reshape(bp, tp - pad, N_HEADS, HEAD_DIM))
        outs["ks"].append(kb_s.reshape(bs, ts, N_HEADS, HEAD_DIM))
        outs["vs"].append(vb_s.reshape(bs, ts, N_HEADS, HEAD_DIM))
        outs["sp"].append(s_p); outs["ss"].append(s_s)
        outs["cp"].append(c_p); outs["cs"].append(c_s)

    y = _final_norm(x, lnf_w[None], ROW_TILE)
    y_prompt = y[:np_rows].reshape(bp, tp, d)[:, pad + N_META:]
    y_sample = y[np_rows:].reshape(bs, ts, d)
    stack = lambda key: jnp.stack(outs[key])
    return (y_prompt, y_sample, stack("kp"), stack("vp"), stack("ks"), stack("vs"),
            stack("sp"), stack("ss"), stack("cp"), stack("cs"))
```
